```python
import math
import jax
import jax.numpy as jnp
from jax import lax
import numpy as np

D_MODEL = 1024
BATCH = 16
SEQ = 256
DEPTH = 4
DEC_BATCH = 8
DEC_SEQ = 1024
PAST_LEN = 256

GRID_W = 64
N_EVEN = (DEPTH + 1) // 2
N_ODD = DEPTH // 2
HEAD_DIM = 128
MIX_HALF = D_MODEL // 2
DN_HEADS = MIX_HALF // HEAD_DIM
DN_CHUNK = 64
SHORT_CONV = 3
HY_WIDTH = MIX_HALF
HY_ORDER = 2
HY_BANDS = 16
HY_EMB_DIM = 1 + 2 * HY_BANDS
HY_FFN = 64
HY_DECAY_PCT_SHORT = 0.3
HY_DECAY_PCT_LONG = 1.5
HY_TARGET = 1e-2
C_HEADS = MIX_HALF // HEAD_DIM
C_KV = C_HEADS // 2
D_HEADS = MIX_HALF // HEAD_DIM
D_KV = D_HEADS // 2
WINDOW = 128
Q_BLOCK = 128
ROPE_THETA = 10000.0
D_FF = 2816
FFN_CONV = 3
EPS = 1e-6
NEG_BIG = -1e30
EVEN_IN = 4 * DN_HEADS * HEAD_DIM + 4 * DN_HEADS + 3 * HY_WIDTH
ODD_IN = (C_HEADS + 2 * C_KV + D_HEADS + 2 * D_KV) * HEAD_DIM
F32 = jnp.float32

kernel_name = 'hybrid_diffusion_prefix_step'


def rmsnorm(x, g):
    xf = x.astype(F32)
    y = xf * lax.rsqrt(jnp.mean(xf * xf, axis=-1, keepdims=True) + EPS)
    return (y * g.astype(F32)).astype(x.dtype)


def l2norm(x):
    return x * lax.rsqrt(jnp.sum(x * x, axis=-1, keepdims=True) + EPS)


def dwconv(x, w, b=None):
    k = w.shape[0]
    pad = k // 2
    L = x.shape[1]
    xp = jnp.pad(x, ((0, 0), (pad, pad), (0, 0)))
    y = xp[:, 0:L] * w[0]
    for i in range(1, k):
        y = y + xp[:, i:i + L] * w[i]
    return y if b is None else y + b


def modulation(cvec, w_mod, b_mod):
    m = jax.nn.silu(cvec) @ w_mod + b_mod
    return jnp.split(m[:, None, :], 6, axis=-1)


def gated_delta_chunked(q, k, v, g, beta, s0):
    B, L, H, DK = q.shape
    DV = v.shape[-1]
    C = DN_CHUNK
    N = L // C

    def chunks(t):
        t = t.reshape((B, N, C, H) + t.shape[3:])
        return jnp.moveaxis(t, (1, 3), (0, 2))

    qc, kc, vc, gc, bc = chunks(q), chunks(k), chunks(v), chunks(g), chunks(beta)
    gcum = jnp.cumsum(gc, axis=-1)
    idx = jnp.arange(C)
    tril = idx[:, None] >= idx[None, :]
    decay = jnp.exp(jnp.where(tril, gcum[..., :, None] - gcum[..., None, :], -jnp.inf))
    kb = kc * bc[..., None]
    a = jnp.einsum('nbhid,nbhjd->nbhij', kb, kc) * decay
    a = jnp.where(idx[:, None] > idx[None, :], a, 0.0)
    m = a + jnp.eye(C, dtype=a.dtype)
    rhs = jnp.concatenate([vc * bc[..., None], kb * jnp.exp(gcum)[..., None]], axis=-1)
    sol = lax.linalg.triangular_solve(m, rhs, left_side=True, lower=True, unit_diagonal=True)
    u, w = sol[..., :DV], sol[..., DV:]
    qg = qc * jnp.exp(gcum)[..., None]
    kdec = kc * jnp.exp(gcum[..., -1:] - gcum)[..., None]
    glast = jnp.exp(gcum[..., -1])
    attn = jnp.einsum('nbhid,nbhjd->nbhij', qc, kc) * decay

    def step(s, xs):
        qg_i, kdec_i, u_i, w_i, attn_i, gl_i = xs
        v_new = u_i - jnp.einsum('bhcd,bhde->bhce', w_i, s)
        o = jnp.einsum('bhcd,bhde->bhce', qg_i, s) + jnp.einsum('bhij,bhje->bhie', attn_i, v_new)
        s = s * gl_i[..., None, None] + jnp.einsum('bhcd,bhce->bhde', kdec_i, v_new)
        return s, o

    s_fin, o = lax.scan(step, s0, (qg, kdec, u, w, attn, glast))
    o = jnp.moveaxis(o, (0, 2), (1, 3)).reshape(B, L, H, DV)
    return o, s_fin


def deltanet_bidir(q, k, v, beta, g, s0):
    o_f, s_f = gated_delta_chunked(q, k, v, g[:, :, 0], beta[:, :, 0], s0[:, 0])
    fl = lambda t: jnp.flip(t, axis=1)
    o_b, s_b = gated_delta_chunked(fl(q), fl(k), fl(v), fl(g[:, :, 1]), fl(beta[:, :, 1]), s0[:, 1])
    return o_f + fl(o_b), jnp.stack([s_f, s_b], axis=1)


def hyena_filters(L, p, j):
    t = jnp.linspace(0.0, 1.0, L, dtype=F32)[:, None]
    w = (2.0 * math.pi / L) * jnp.arange(L, dtype=F32)[:, None]
    f = jnp.linspace(1e-4, HY_BANDS - 1, HY_BANDS, dtype=F32)[None, :]
    feats = jnp.concatenate([t, jnp.cos(f * w), -jnp.sin(f * w)], axis=-1)
    z = jnp.sin(p['hy_freq1'][j].astype(F32) * (feats @ p['hy_w1'][j].astype(F32) + p['hy_b1'][j].astype(F32)))
    z = jnp.sin(p['hy_freq2'][j].astype(F32) * (z @ p['hy_w2'][j].astype(F32) + p['hy_b2'][j].astype(F32)))
    h = z @ p['hy_w3'][j].astype(F32)
    deltas = jnp.abs(jnp.linspace(math.log(HY_TARGET) / HY_DECAY_PCT_LONG,
                                  math.log(HY_TARGET) / HY_DECAY_PCT_SHORT, HY_WIDTH, dtype=F32))
    h = h.reshape(L, HY_ORDER, 2, HY_WIDTH) * jnp.exp(-t * deltas)[:, None, None, :]
    return jnp.transpose(h, (1, 2, 0, 3))


def long_conv_bidir(u, h_fwd, h_bwd, bias):
    L = u.shape[1]
    taps = jnp.concatenate([h_fwd, jnp.zeros_like(h_fwd[:1]), jnp.flip(h_bwd[1:], axis=0)], axis=0)
    uf = jnp.fft.rfft(u, n=2 * L, axis=1)
    hf = jnp.fft.rfft(taps, n=2 * L, axis=0)
    y = jnp.fft.irfft(uf * hf[None], n=2 * L, axis=1)[:, :L]
    return y + u * bias


def hyena_mixer(xh, conv_w, conv_b, filt, bias):
    u = dwconv(xh, conv_w, conv_b).astype(F32)
    x1, x2, z = jnp.split(u, 3, axis=-1)
    for o, gate in enumerate((x1, x2)):
        z = gate * long_conv_bidir(z, filt[o, 0], filt[o, 1], bias[o])
    return z


def even_mixer(h, p, j, s0):
    B, L, _ = h.shape
    ha = DN_HEADS * HEAD_DIM
    proj = h @ p['ev_w_in'][j]
    qkv, z, b_raw, a_raw, hy_in = jnp.split(
        proj, [3 * ha, 4 * ha, 4 * ha + 2 * DN_HEADS, 4 * ha + 4 * DN_HEADS], axis=-1)
    qkv = jax.nn.silu(dwconv(qkv, p['dn_conv_w'][j])).astype(F32)
    q, k, v = [t.reshape(B, L, DN_HEADS, HEAD_DIM) for t in jnp.split(qkv, 3, axis=-1)]
    q = l2norm(q) * (HEAD_DIM ** -0.5)
    k = l2norm(k)
    beta = jax.nn.sigmoid(b_raw.astype(F32)).reshape(B, L, 2, DN_HEADS)
    g = -jnp.exp(p['dn_a_log'][j].astype(F32)) * jax.nn.softplus(
        a_raw.astype(F32).reshape(B, L, 2, DN_HEADS) + p['dn_dt_bias'][j].astype(F32))
    o, s_fin = deltanet_bidir(q, k, v, beta, g, s0.astype(F32))
    o = rmsnorm(o, p['dn_norm'][j]) * jax.nn.silu(z.astype(F32).reshape(B, L, DN_HEADS, HEAD_DIM))
    filt = hyena_filters(L, p, j)
    yb = hyena_mixer(hy_in, p['hy_conv_w'][j], p['hy_conv_b'][j], filt, p['hy_bias'][j].astype(F32))
    mix = jnp.concatenate([o.reshape(B, L, ha), yb], axis=-1)
    return mix.astype(h.dtype), s_fin


def rope_2d_tables(L):
    rows = L // GRID_W
    row = jnp.repeat(jnp.arange(rows), GRID_W).astype(F32)
    col = (jnp.arange(rows * GRID_W) % GRID_W).astype(F32)
    half = HEAD_DIM // 2
    inv = ROPE_THETA ** (-jnp.arange(0, half, 2, dtype=F32) / half)
    ang = jnp.concatenate([row[:, None] * inv, col[:, None] * inv], axis=-1)
    return jnp.cos(ang), jnp.sin(ang)


def apply_rope_2d(x, cos, sin):
    B, L, H, HD = x.shape
    xf = x.astype(F32).reshape(B, L, H, 2, 2, HD // 4)
    x1, x2 = xf[..., 0, :], xf[..., 1, :]
    c = cos.reshape(L, 2, HD // 4)[None, :, None]
    s = sin.reshape(L, 2, HD // 4)[None, :, None]
    out = jnp.stack([x1 * c - x2 * s, x2 * c + x1 * s], axis=-2).reshape(B, L, H, HD)
    return out.astype(x.dtype)


def softmax_with_sink(s, sink):
    if sink is None:
        return jax.nn.softmax(s, axis=-1)
    sk = sink.astype(F32).reshape(s.shape[-4], s.shape[-3], 1, 1)
    m = jnp.maximum(jnp.max(s, axis=-1, keepdims=True), sk)
    e = jnp.exp(s - m)
    return e / (jnp.sum(e, axis=-1, keepdims=True) + jnp.exp(sk - m))


def dense_attention(q, k, v, sink):
    B, Lq, H, HD = q.shape
    KV = k.shape[2]
    G = H // KV
    nb = Lq // Q_BLOCK
    qb = q.reshape(B, nb, Q_BLOCK, KV, G, HD).transpose(1, 0, 2, 3, 4, 5)
    scale = HD ** -0.5

    def one_block(qi):
        s = jnp.einsum('bqkgd,bskd->bkgqs', qi, k, preferred_element_type=F32) * scale
        pr = softmax_with_sink(s, sink)
        return jnp.einsum('bkgqs,bskd->bqkgd', pr.astype(v.dtype), v)

    out = lax.map(one_block, qb)
    return out.transpose(1, 0, 2, 3, 4, 5).reshape(B, Lq, H, HD)


def banded_attention(q, k, v, ck, cv, sink):
    B, L, H, HD = q.shape
    KV = k.shape[2]
    G = H // KV
    W = WINDOW
    nb = L // W
    qb = q.reshape(B, nb, W, KV, G, HD)

    def band(t):
        tp = jnp.pad(t, ((0, 0), (W, W), (0, 0), (0, 0))).reshape(B, nb + 2, W, KV, HD)
        return jnp.concatenate([tp[:, :-2], tp[:, 1:-1], tp[:, 2:]], axis=2)

    kb, vb = band(k), band(v)
    qi = jnp.arange(W)[None, :, None]
    kj = jnp.arange(3 * W)[None, None, :]
    blk = jnp.arange(nb)[:, None, None]
    keypos = blk * W - W + kj
    valid = (jnp.abs(kj - W - qi) <= W) & (keypos >= 0) & (keypos < L)
    scale = HD ** -0.5
    s_loc = jnp.einsum('bnqkgd,bnskd->bnkgqs', qb, kb, preferred_element_type=F32) * scale
    s_loc = jnp.where(valid[None, :, None, None], s_loc, NEG_BIG)
    s_ctx = jnp.einsum('bnqkgd,bskd->bnkgqs', qb, ck, preferred_element_type=F32) * scale
    pr = softmax_with_sink(jnp.concatenate([s_loc, s_ctx], axis=-1), sink)
    out = (jnp.einsum('bnkgqs,bnskd->bnqkgd', pr[..., :3 * W].astype(v.dtype), vb)
           + jnp.einsum('bnkgqs,bskd->bnqkgd', pr[..., 3 * W:].astype(v.dtype), cv))
    return out.reshape(B, L, H, HD)


def odd_mixer(h, p, j, ctx, rope):
    B, L, _ = h.shape
    proj = h @ p['od_w_in'][j]
    cuts = [int(x) * HEAD_DIM for x in np.cumsum([C_HEADS, C_KV, C_KV, D_HEADS, D_KV])]
    qc, kc, vc, qd, kd, vd = jnp.split(proj, cuts, axis=-1)
    heads = lambda t: t.reshape(B, L, -1, HEAD_DIM)
    qc = rmsnorm(heads(qc), p['c_q_norm'][j])
    kc = rmsnorm(heads(kc), p['c_k_norm'][j])
    vc, qd, kd, vd = heads(vc), heads(qd), heads(kd), heads(vd)
    sink = p['d_sink'][j]
    if ctx is None:
        yc = dense_attention(qc, kc, vc, None)
        yd = dense_attention(qd, kd, vd, sink)
        new = (kc, vc, kd, vd)
    else:
        cos, sin = rope
        qc, kc, qd, kd = [apply_rope_2d(t, cos, sin) for t in (qc, kc, qd, kd)]
        ck_c, cv_c, ck_d, cv_d = ctx
        yc = dense_attention(qc, jnp.concatenate([ck_c, kc], axis=1), jnp.concatenate([cv_c, vc], axis=1), None)
        yd = banded_attention(qd, kd, vd, ck_d, cv_d, sink)
        new = None
    mix = jnp.concatenate([yc.reshape(B, L, -1), yd.reshape(B, L, -1)], axis=-1)
    return mix.astype(h.dtype), new


def conv_ffn(h, w_up, conv_w, conv_b, w_down):
    a, b = jnp.split(h @ w_up, 2, axis=-1)
    a = dwconv(a, conv_w, conv_b)
    return (jax.nn.silu(a) * b) @ w_down


def run_trunk(x, cvec, p, past):
    is_ctx = past is None
    B, L, _ = x.shape
    rope = None if is_ctx else rope_2d_tables(L)
    dn_states, k_c, v_c, k_d, v_d = [], [], [], [], []
    for i in range(DEPTH):
        j = i // 2
        sh1, sc1, g1, sh2, sc2, g2 = modulation(cvec, p['w_mod'][i], p['b_mod'][i])
        h = rmsnorm(x, p['norm_mix'][i]) * (1.0 + sc1) + sh1
        if i % 2 == 0:
            if is_ctx:
                s0 = jnp.zeros((B, 2, DN_HEADS, HEAD_DIM, HEAD_DIM), F32)
            else:
                s0 = past[0][:, j]
            mix, s_fin = even_mixer(h, p, j, s0)
            if is_ctx:
                dn_states.append(s_fin.astype(x.dtype))
        else:
            ctx = None if is_ctx else (past[1][:, j], past[2][:, j], past[3][:, j], past[4][:, j])
            mix, kv = odd_mixer(h, p, j, ctx, rope)
            if is_ctx:
                k_c.append(kv[0])
                v_c.append(kv[1])
                k_d.append(kv[2])
                v_d.append(kv[3])
        x = x + g1 * (mix @ p['w_out'][i])
        h = rmsnorm(x, p['norm_ffn'][i]) * (1.0 + sc2) + sh2
        x = x + g2 * conv_ffn(h, p['ffn_w_up'][i], p['ffn_conv_w'][i], p['ffn_conv_b'][i], p['ffn_w_down'][i])
    y = rmsnorm(x, p['final_norm'])
    if not is_ctx:
        return y, None
    return y, (jnp.stack(dn_states, axis=1), jnp.stack(k_c, axis=1), jnp.stack(v_c, axis=1),
               jnp.stack(k_d, axis=1), jnp.stack(v_d, axis=1))


def setup_inputs(seed: int = 0) -> dict:
    key = jax.random.key(seed)
    ks = jax.random.split(key, 64)
    counter = [0]

    def nk():
        counter[0] += 1
        return ks[counter[0] - 1]

    def nrm(shape, scale):
        return jax.random.normal(nk(), shape, F32) * scale

    def gain(shape):
        return 1.0 + nrm(shape, 0.02)

    D = D_MODEL
    ha = DN_HEADS * HEAD_DIM
    dt = jnp.exp(jax.random.uniform(nk(), (N_EVEN, 2, DN_HEADS), F32, math.log(1e-3), math.log(1e-1)))
    a_log = jnp.log(jax.random.uniform(nk(), (N_EVEN, 2, DN_HEADS), F32, 1.0, 16.0))
    return {
        'x_prompt': nrm((BATCH, SEQ, D), 1.0),
        'x_sample': nrm((DEC_BATCH, DEC_SEQ, D), 1.0),
        'state_dn': nrm((DEC_BATCH, N_EVEN, 2, DN_HEADS, HEAD_DIM, HEAD_DIM), 0.1),
        'cache_k_c': nrm((DEC_BATCH, N_ODD, PAST_LEN, C_KV, HEAD_DIM), 1.0),
        'cache_v_c': nrm((DEC_BATCH, N_ODD, PAST_LEN, C_KV, HEAD_DIM), 1.0),
        'cache_k_d': nrm((DEC_BATCH, N_ODD, PAST_LEN, D_KV, HEAD_DIM), 1.0),
        'cache_v_d': nrm((DEC_BATCH, N_ODD, PAST_LEN, D_KV, HEAD_DIM), 1.0),
        'c': nrm((DEC_BATCH, D), 1.0),
        'c_ctx': nrm((D,), 1.0),
        'final_norm': gain((D,)),
        'w_mod': nrm((DEPTH, D, 6 * D), 0.5 * D ** -0.5),
        'b_mod': nrm((DEPTH, 6 * D), 0.02),
        'norm_mix': gain((DEPTH, D)),
        'norm_ffn': gain((DEPTH, D)),
        'w_out': nrm((DEPTH, D, D), D ** -0.5),
        'ffn_w_up': nrm((DEPTH, D, 2 * D_FF), D ** -0.5),
        'ffn_conv_w': nrm((DEPTH, FFN_CONV, D_FF), FFN_CONV ** -0.5),
        'ffn_conv_b': nrm((DEPTH, D_FF), 0.02),
        'ffn_w_down': nrm((DEPTH, D_FF, D), D_FF ** -0.5),
        'ev_w_in': nrm((N_EVEN, D, EVEN_IN), D ** -0.5),
        'dn_conv_w': nrm((N_EVEN, SHORT_CONV, 3 * ha), SHORT_CONV ** -0.5),
        'dn_a_log': a_log,
        'dn_dt_bias': dt + jnp.log(-jnp.expm1(-dt)),
        'dn_norm': gain((N_EVEN, HEAD_DIM)),
        'hy_conv_w': nrm((N_EVEN, 3, 3 * HY_WIDTH), 3 ** -0.5),
        'hy_conv_b': nrm((N_EVEN, 3 * HY_WIDTH), 0.02),
        'hy_w1': nrm((N_EVEN, HY_EMB_DIM, HY_FFN), HY_EMB_DIM ** -0.5),
        'hy_b1': nrm((N_EVEN, HY_FFN), 0.1),
        'hy_freq1': gain((N_EVEN, HY_FFN)),
        'hy_w2': nrm((N_EVEN, HY_FFN, HY_FFN), HY_FFN ** -0.5),
        'hy_b2': nrm((N_EVEN, HY_FFN), 0.1),
        'hy_freq2': gain((N_EVEN, HY_FFN)),
        'hy_w3': nrm((N_EVEN, HY_FFN, HY_ORDER * 2 * HY_WIDTH), 0.1 * HY_FFN ** -0.5),
        'hy_bias': nrm((N_EVEN, HY_ORDER, HY_WIDTH), 0.1),
        'od_w_in': nrm((N_ODD, D, ODD_IN), D ** -0.5),
        'c_q_norm': gain((N_ODD, HEAD_DIM)),
        'c_k_norm': gain((N_ODD, HEAD_DIM)),
        'd_sink': nrm((N_ODD, D_HEADS), 0.5),
    }


def reference(x_prompt, x_sample, state_dn, cache_k_c, cache_v_c, cache_k_d, cache_v_d, c,
              c_ctx, final_norm, w_mod, b_mod, norm_mix, norm_ffn, w_out, ffn_w_up, ffn_conv_w,
              ffn_conv_b, ffn_w_down, ev_w_in, dn_conv_w, dn_a_log, dn_dt_bias, dn_norm,
              hy_conv_w, hy_conv_b, hy_w1, hy_b1, hy_freq1, hy_w2, hy_b2, hy_freq2, hy_w3, hy_bias,
              od_w_in, c_q_norm, c_k_norm, d_sink):
    p = {
        'final_norm': final_norm, 'w_mod': w_mod, 'b_mod': b_mod, 'norm_mix': norm_mix,
        'norm_ffn': norm_ffn, 'w_out': w_out, 'ffn_w_up': ffn_w_up, 'ffn_conv_w': ffn_conv_w,
        'ffn_conv_b': ffn_conv_b, 'ffn_w_down': ffn_w_down, 'ev_w_in': ev_w_in,
        'dn_conv_w': dn_conv_w, 'dn_a_log': dn_a_log, 'dn_dt_bias': dn_dt_bias, 'dn_norm': dn_norm,
        'hy_conv_w': hy_conv_w, 'hy_conv_b': hy_conv_b, 'hy_w1': hy_w1, 'hy_b1': hy_b1,
        'hy_freq1': hy_freq1, 'hy_w2': hy_w2, 'hy_b2': hy_b2, 'hy_freq2': hy_freq2,
        'hy_w3': hy_w3, 'hy_bias': hy_bias, 'od_w_in': od_w_in, 'c_q_norm': c_q_norm,
        'c_k_norm': c_k_norm, 'd_sink': d_sink,
    }
    y_prompt, ctx_state = run_trunk(x_prompt, c_ctx[None, :], p, None)
    new_state_dn, new_k_c, new_v_c, new_k_d, new_v_d = ctx_state
    y_sample, _ = run_trunk(x_sample, c, p, (state_dn, cache_k_c, cache_v_c, cache_k_d, cache_v_d))
    return (y_prompt, y_sample, new_state_dn, new_k_c, new_v_c, new_k_d, new_v_d)
```

```python
import functools
import math

import jax
import jax.numpy as jnp
import numpy as np
from jax import lax
from jax.experimental import pallas as pl
from jax.experimental.pallas import tpu as pltpu

F32 = jnp.float32
BF16 = jnp.bfloat16

D_MODEL = 1024
BATCH = 16
SEQ = 256
DEPTH = 4
DEC_BATCH = 8
DEC_SEQ = 1024
PAST_LEN = 256
GRID_W = 64
HEAD_DIM = 128
MIX_HALF = D_MODEL // 2
DN_HEADS = MIX_HALF // HEAD_DIM
DN_CHUNK = 64
HY_WIDTH = MIX_HALF
HY_BANDS = 16
HY_FFN = 64
HY_DECAY_PCT_SHORT = 0.3
HY_DECAY_PCT_LONG = 1.5
HY_TARGET = 1e-2
KV_HEADS = 2
Q_GROUP = 2
WINDOW = 128
ROPE_THETA = 10000.0
D_FF = 2816
EPS = 1e-6
NEG_BIG = -1e30

T_CTX = BATCH * SEQ
T_LAT = DEC_BATCH * DEC_SEQ
T_ALL = T_CTX + T_LAT
MOD_ROWS = 16
LANES = 128
VMEM_LIMIT = 56 * 1024 * 1024
FF_CHUNK = 256
N_FF_CHUNKS = D_FF // FF_CHUNK
HY_GROUP = 256
EVEN_COLS = 3 * MIX_HALF + 3 * HY_WIDTH + MIX_HALF + LANES
ODD_COLS = 2048

HIGHEST = lax.Precision.HIGHEST


def _params(sem):
    return pltpu.CompilerParams(dimension_semantics=sem, vmem_limit_bytes=VMEM_LIMIT)


def _sigmoid(x):
    return 1.0 / (1.0 + jnp.exp(-x))


def _silu(x):
    return x * _sigmoid(x)


def _softplus(x):
    return jnp.maximum(x, 0.0) + jnp.log1p(jnp.exp(-jnp.abs(x)))


def _dot(a, b):
    return jnp.dot(a.astype(BF16), b.astype(BF16), preferred_element_type=F32)


def _dot_nt(a, b):
    return lax.dot_general(a.astype(BF16), b.astype(BF16), (((1,), (1,)), ((), ())),
                           preferred_element_type=F32)


def _dot_tn(a, b):
    return lax.dot_general(a.astype(BF16), b.astype(BF16), (((0,), (0,)), ((), ())),
                           preferred_element_type=F32)


def _dot_hi(a, b):
    return jnp.dot(a, b, preferred_element_type=F32, precision=HIGHEST)


def _rms(x, gain):
    return x * lax.rsqrt(jnp.mean(x * x, axis=-1, keepdims=True) + EPS) * gain


def _mod_row(i, tile):
    n_ctx = T_CTX // tile
    return jnp.where(i < n_ctx, 0, 1 + (i - n_ctx) // (DEC_SEQ // tile))


def _mod_kernel(c_ref, w_ref, b_ref, o_ref):
    o_ref[0] = _dot(_silu(c_ref[...]), w_ref[0]) + b_ref[0]


def _modulation(cvec, w_mod, b_mod):
    n_tile = 1536
    return pl.pallas_call(
        _mod_kernel,
        grid=(DEPTH, 6 * D_MODEL // n_tile),
        in_specs=[
            pl.BlockSpec((MOD_ROWS, D_MODEL), lambda l, n: (0, 0)),
            pl.BlockSpec((1, D_MODEL, n_tile), lambda l, n: (l, 0, n)),
            pl.BlockSpec((1, 1, n_tile), lambda l, n: (l, 0, n)),
        ],
        out_specs=pl.BlockSpec((1, MOD_ROWS, n_tile), lambda l, n: (l, 0, n)),
        out_shape=jax.ShapeDtypeStruct((DEPTH, MOD_ROWS, 6 * D_MODEL), F32),
        compiler_params=_params(("arbitrary", "arbitrary")),
        name="modulation",
    )(cvec, w_mod, b_mod.reshape(DEPTH, 1, 6 * D_MODEL))


def _inproj_kernel(x_ref, mod_ref, gain_ref, w_ref, o_ref, *, n_chunk):
    m = mod_ref[0]
    h = _rms(x_ref[...], gain_ref[...]) * (1.0 + m[1:2]) + m[0:1]
    hb = h.astype(BF16)
    n_cols = o_ref.shape[1]
    for c0 in range(0, n_cols, n_chunk):
        c1 = min(c0 + n_chunk, n_cols)
        o_ref[:, c0:c1] = jnp.dot(hb, w_ref[:, c0:c1], preferred_element_type=F32)


def _inproj(x, mod, gain, w):
    tile = 512
    n_cols = w.shape[1]
    return pl.pallas_call(
        functools.partial(_inproj_kernel, n_chunk=512),
        grid=(T_ALL // tile,),
        in_specs=[
            pl.BlockSpec((tile, D_MODEL), lambda i: (i, 0)),
            pl.BlockSpec((1, 6, D_MODEL), lambda i: (_mod_row(i, tile), 0, 0)),
            pl.BlockSpec((1, D_MODEL), lambda i: (0, 0)),
            pl.BlockSpec((D_MODEL, n_cols), lambda i: (0, 0)),
        ],
        out_specs=pl.BlockSpec((tile, n_cols), lambda i: (i, 0)),
        out_shape=jax.ShapeDtypeStruct((T_ALL, n_cols), F32),
        compiler_params=_params(("arbitrary",)),
        name="inproj",
    )(x, mod, gain.reshape(1, D_MODEL), w)


def _conv3(x, w, first, last, seq_rows):
    xp = jnp.where(first, 0.0, pltpu.roll(x, 1, 0))
    xn = jnp.where(last, 0.0, pltpu.roll(x, seq_rows - 1, 0))
    return xp * w[0:1] + x * w[1:2] + xn * w[2:3]


def _dn_kernel(*refs, seq, use_s0, write_state):
    refs = list(refs)
    q_ref, k_ref, v_ref, z_ref, ba_ref, cwq_ref, cwk_ref, cwv_ref, par_ref, norm_ref = refs[:10]
    pos = 10
    s0_ref = None
    if use_s0:
        s0_ref = refs[pos]
        pos += 1
    o_ref = refs[pos]
    pos += 1
    sfin_ref = None
    if write_state:
        sfin_ref = refs[pos]
        pos += 1
    q_s, k_s, v_s, o_s, g_s, gr_s, beta_s, st_s = refs[pos:]

    C = DN_CHUNK
    n_chunks = seq // C
    hd = HEAD_DIM

    rows = lax.broadcasted_iota(jnp.int32, (seq, 1), 0)
    first = rows == 0
    last = rows == seq - 1
    for h in range(DN_HEADS):
        hs = slice(h * hd, (h + 1) * hd)
        q = _silu(_conv3(q_ref[:, hs], cwq_ref[:, hs], first, last, seq))
        k = _silu(_conv3(k_ref[:, hs], cwk_ref[:, hs], first, last, seq))
        v = _silu(_conv3(v_ref[:, hs], cwv_ref[:, hs], first, last, seq))
        q_s[:, hs] = q * lax.rsqrt(jnp.sum(q * q, axis=-1, keepdims=True) + EPS) * (hd ** -0.5)
        k_s[:, hs] = k * lax.rsqrt(jnp.sum(k * k, axis=-1, keepdims=True) + EPS)
        v_s[:, hs] = v

    ba = ba_ref[...]
    beta_s[...] = _sigmoid(ba)
    g = -jnp.exp(par_ref[0:1]) * _softplus(ba + par_ref[1:2])
    lane = lax.broadcasted_iota(jnp.int32, (1, LANES), 1)
    fwd_lane = lane < 8 + DN_HEADS
    ci = lax.broadcasted_iota(jnp.int32, (C, C), 0)
    cj = lax.broadcasted_iota(jnp.int32, (C, C), 1)
    tril1 = (ci >= cj).astype(F32)
    triu1 = (ci <= cj).astype(F32)
    for n in range(n_chunks):
        gn = g[n * C:(n + 1) * C]
        gc = jnp.where(fwd_lane, _dot_hi(tril1, gn), _dot_hi(triu1, gn))
        g_s[n * C:(n + 1) * C] = gc
        gr_s[n] = gc.T

    if use_s0:
        for d in range(2):
            for h in range(DN_HEADS):
                st_s[d * DN_HEADS + h] = s0_ref[0, d, h]
    else:
        st_s[...] = jnp.zeros_like(st_s)
    o_s[...] = jnp.zeros_like(o_s)

    eye = (ci == cj).astype(F32)

    def step(t, carry):
        for d in range(2):
            n = t if d == 0 else n_chunks - 1 - t
            r0 = pl.multiple_of(n * C, C)
            incl = (ci >= cj) if d == 0 else (ci <= cj)
            strict = (ci > cj) if d == 0 else (ci < cj)
            for h in range(DN_HEADS):
                p = d * DN_HEADS + h
                hs = slice(h * hd, (h + 1) * hd)
                qn = q_s[pl.ds(r0, C), hs]
                kn = k_s[pl.ds(r0, C), hs]
                vn = v_s[pl.ds(r0, C), hs]
                gcol = g_s[pl.ds(r0, C), 8 + p:9 + p]
                grow = gr_s[n, 8 + p:9 + p, :]
                bcol = beta_s[pl.ds(r0, C), p:p + 1]
                r_last = r0 + (C - 1 if d == 0 else 0)
                glast = g_s[pl.ds(r_last, 1), 8 + p:9 + p]
                decay = jnp.exp(jnp.where(incl, gcol - grow, -jnp.inf))
                kb = kn * bcol
                a = jnp.where(strict, _dot_nt(kb, kn) * decay, 0.0)
                pw = -a
                inv = eye + pw
                for _ in range(5):
                    pw = _dot_hi(pw, pw)
                    inv = inv + _dot_hi(inv, pw)
                egc = jnp.exp(gcol)
                rhs = jnp.concatenate([vn * bcol, kb * egc], axis=1)
                sol = _dot_hi(inv, rhs)
                u = sol[:, :hd]
                w = sol[:, hd:]
                qg = qn * egc
                kdec = kn * jnp.exp(glast - gcol)
                attn = _dot_nt(qn, kn) * decay
                s = st_s[p]
                v_new = u - _dot(w, s)
                o = _dot(qg, s) + _dot(attn, v_new)
                st_s[p] = s * jnp.exp(glast) + _dot_tn(kdec, v_new)
                o_s[pl.ds(r0, C), hs] += o
        return carry

    lax.fori_loop(0, n_chunks, step, 0)

    for h in range(DN_HEADS):
        hs = slice(h * hd, (h + 1) * hd)
        o_ref[:, hs] = _rms(o_s[:, hs], norm_ref[...]) * _silu(z_ref[:, hs])
    if write_state:
        for d in range(2):
            for h in range(DN_HEADS):
                sfin_ref[0, d, h] = st_s[d * DN_HEADS + h]


def _deltanet(proj, conv_w, par, norm, s0, prev_out, *, seq, n_seq, row0):
    use_s0 = s0 is not None
    write_state = not use_s0
    blk0 = row0 // seq
    half = MIX_HALF
    in_specs = [
        pl.BlockSpec((seq, half), lambda b: (blk0 + b, 0)),
        pl.BlockSpec((seq, half), lambda b: (blk0 + b, 1)),
        pl.BlockSpec((seq, half), lambda b: (blk0 + b, 2)),
        pl.BlockSpec((seq, half), lambda b: (blk0 + b, 6)),
        pl.BlockSpec((seq, LANES), lambda b: (blk0 + b, 28)),
        pl.BlockSpec((3, half), lambda b: (0, 0)),
        pl.BlockSpec((3, half), lambda b: (0, 1)),
        pl.BlockSpec((3, half), lambda b: (0, 2)),
        pl.BlockSpec((8, LANES), lambda b: (0, 0)),
        pl.BlockSpec((1, HEAD_DIM), lambda b: (0, 0)),
    ]
    args = [proj, proj, proj, proj, proj, conv_w, conv_w, conv_w, par, norm]
    if use_s0:
        in_specs.append(pl.BlockSpec((1, 2, DN_HEADS, HEAD_DIM, HEAD_DIM), lambda b: (b, 0, 0, 0, 0)))
        args.append(s0)
    aliases = {}
    if prev_out is not None:
        in_specs.append(pl.BlockSpec(memory_space=pl.ANY))
        args.append(prev_out)
        aliases = {len(args) - 1: 0}
    out_specs = [pl.BlockSpec((seq, half), lambda b: (blk0 + b, 0))]
    out_shape = [jax.ShapeDtypeStruct((T_ALL, half), F32)]
    if write_state:
        out_specs.append(pl.BlockSpec((1, 2, DN_HEADS, HEAD_DIM, HEAD_DIM), lambda b: (b, 0, 0, 0, 0)))
        out_shape.append(jax.ShapeDtypeStruct((n_seq, 2, DN_HEADS, HEAD_DIM, HEAD_DIM), F32))

    def body(*refs):
        refs = list(refs)
        if prev_out is not None:
            n_in = len(args)
            del refs[n_in - 1]
        _dn_kernel(*refs, seq=seq, use_s0=use_s0, write_state=write_state)

    n_chunks = seq // DN_CHUNK
    res = pl.pallas_call(
        body,
        grid=(n_seq,),
        in_specs=in_specs,
        out_specs=out_specs,
        out_shape=out_shape,
        scratch_shapes=[
            pltpu.VMEM((seq, half), F32),
            pltpu.VMEM((seq, half), F32),
            pltpu.VMEM((seq, half), F32),
            pltpu.VMEM((seq, half), F32),
            pltpu.VMEM((seq, LANES), F32),
            pltpu.VMEM((n_chunks, LANES, DN_CHUNK), F32),
            pltpu.VMEM((seq, LANES), F32),
            pltpu.VMEM((2 * DN_HEADS, HEAD_DIM, HEAD_DIM), F32),
        ],
        input_output_aliases=aliases,
        compiler_params=_params(("arbitrary",)),
        name="deltanet_%d" % seq,
    )(*args)
    return res


@functools.lru_cache(maxsize=None)
def _dft_tables(seq):
    k = np.arange(seq, dtype=np.int64)[:, None]
    s = np.arange(seq, dtype=np.int64)[None, :]
    ang = np.pi * ((k * s) % (2 * seq)).astype(np.float64) / seq
    cos = np.cos(ang)
    sin = np.sin(ang)
    fwd = np.concatenate([cos, -sin], axis=0)
    fwd[seq, :] = np.where(np.arange(seq) % 2 == 0, 1.0, -1.0)
    wk = np.full((seq,), 2.0)
    wk[0] = 1.0
    inv = np.concatenate([cos.T * wk[None, :], -2.0 * sin.T], axis=1) / (2.0 * seq)
    inv[:, seq] = np.where(np.arange(seq) % 2 == 0, 1.0, -1.0) / (2.0 * seq)
    return fwd.astype(np.float32), inv.astype(np.float32)


def _hy_filter_kernel(feat_ref, w1_ref, b1_ref, f1_ref, w2_ref, b2_ref, f2_ref, w3f_ref, w3b_ref,
                      dec_ref, fwd_ref, o_ref, *, seq):
    z = jnp.sin(f1_ref[...] * (_dot(feat_ref[...], w1_ref[...]) + b1_ref[...]))
    z = jnp.sin(f2_ref[...] * (_dot(z, w2_ref[...]) + b2_ref[...]))
    dec = dec_ref[...]
    h_fwd = _dot(z, w3f_ref[...]) * dec
    h_bwd = _dot(z, w3b_ref[...]) * dec
    rows = lax.broadcasted_iota(jnp.int32, (seq, 1), 0)
    h_bwd = jnp.where(rows == 0, 0.0, h_bwd)
    spec = _dot_hi(fwd_ref[...], jnp.concatenate([h_fwd, h_bwd], axis=1))
    g = h_fwd.shape[1]
    srow = lax.broadcasted_iota(jnp.int32, (2 * seq, 1), 0)
    sign = jnp.where(srow <= seq, 1.0, -1.0)
    o_ref[0] = spec[:, :g] + sign * spec[:, g:]


def _hyena_filters(seq, w1, b1, f1, w2, b2, f2, w3):
    t = jnp.linspace(0.0, 1.0, seq, dtype=F32)[:, None]
    w = (2.0 * math.pi / seq) * jnp.arange(seq, dtype=F32)[:, None]
    f = jnp.linspace(1e-4, HY_BANDS - 1, HY_BANDS, dtype=F32)[None, :]
    feats = jnp.concatenate([t, jnp.cos(f * w), -jnp.sin(f * w)], axis=-1)
    n_feat = feats.shape[1]
    feats = jnp.pad(feats, ((0, 0), (0, LANES - n_feat)))
    w1p = jnp.pad(w1, ((0, LANES - n_feat), (0, 0)))
    deltas = jnp.abs(jnp.linspace(math.log(HY_TARGET) / HY_DECAY_PCT_LONG,
                                  math.log(HY_TARGET) / HY_DECAY_PCT_SHORT, HY_WIDTH, dtype=F32))
    dec = jnp.exp(-t * deltas)
    fwd, _ = _dft_tables(seq)
    g = HY_GROUP
    n_g = HY_WIDTH // g
    small = lambda shape: pl.BlockSpec(shape, lambda o, c: (0,) * len(shape))
    return pl.pallas_call(
        functools.partial(_hy_filter_kernel, seq=seq),
        grid=(2, n_g),
        in_specs=[
            small((seq, LANES)), small((LANES, HY_FFN)), small((1, HY_FFN)), small((1, HY_FFN)),
            small((HY_FFN, HY_FFN)), small((1, HY_FFN)), small((1, HY_FFN)),
            pl.BlockSpec((HY_FFN, g), lambda o, c: (0, (2 * o) * n_g + c)),
            pl.BlockSpec((HY_FFN, g), lambda o, c: (0, (2 * o + 1) * n_g + c)),
            pl.BlockSpec((seq, g), lambda o, c: (0, c)),
            small((2 * seq, seq)),
        ],
        out_specs=pl.BlockSpec((1, 2 * seq, g), lambda o, c: (o, 0, c)),
        out_shape=jax.ShapeDtypeStruct((2, 2 * seq, HY_WIDTH), F32),
        compiler_params=_params(("arbitrary", "arbitrary")),
        name="hyena_filters_%d" % seq,
    )(feats, w1p, b1.reshape(1, -1), f1.reshape(1, -1), w2, b2.reshape(1, -1), f2.reshape(1, -1),
      w3, w3, dec, jnp.asarray(fwd))


def _hy_kernel(x1_ref, x2_ref, z_ref, cw1_ref, cw2_ref, cwz_ref, cb1_ref, cb2_ref, cbz_ref,
               spec_ref, bias_ref, fwd_ref, inv_ref, o_ref, *, seq):
    rows = lax.broadcasted_iota(jnp.int32, (seq, 1), 0)
    first = rows == 0
    last = rows == seq - 1
    gates = (_conv3(x1_ref[...], cw1_ref[...], first, last, seq) + cb1_ref[...],
             _conv3(x2_ref[...], cw2_ref[...], first, last, seq) + cb2_ref[...])
    z = _conv3(z_ref[...], cwz_ref[...], first, last, seq) + cbz_ref[...]
    for o in range(2):
        zf = jnp.dot(fwd_ref[...], z.astype(BF16), preferred_element_type=F32)
        re, im = zf[:seq], zf[seq:]
        hre, him = spec_ref[o, :seq, :], spec_ref[o, seq:, :]
        yre = re * hre - jnp.where(first, 0.0, im * him)
        yim = jnp.where(first, im * him, re * him + im * hre)
        y = jnp.dot(inv_ref[...], jnp.concatenate([yre, yim], axis=0).astype(BF16),
                    preferred_element_type=F32)
        z = gates[o] * (y + z * bias_ref[o:o + 1, :])
    o_ref[...] = z


def _hyena(proj, conv_w, conv_b, spec, bias, prev_out, *, seq, n_seq, row0):
    g = HY_GROUP
    n_g = HY_WIDTH // g
    blk0 = row0 // seq
    col0 = 3 * MIX_HALF // g
    fwd, inv = _dft_tables(seq)
    conv_b = conv_b.reshape(1, -1)

    def pcol(part):
        return pl.BlockSpec((seq, g), lambda c, b: (blk0 + b, col0 + part * n_g + c))

    def wcol(rows_, part):
        return pl.BlockSpec((rows_, g), lambda c, b: (0, part * n_g + c))

    in_specs = [pcol(0), pcol(1), pcol(2), wcol(3, 0), wcol(3, 1), wcol(3, 2),
                wcol(1, 0), wcol(1, 1), wcol(1, 2),
                pl.BlockSpec((2, 2 * seq, g), lambda c, b: (0, 0, c)),
                pl.BlockSpec((2, g), lambda c, b: (0, c)),
                pl.BlockSpec((2 * seq, seq), lambda c, b: (0, 0)),
                pl.BlockSpec((seq, 2 * seq), lambda c, b: (0, 0))]
    args = [proj, proj, proj, conv_w, conv_w, conv_w, conv_b, conv_b, conv_b, spec, bias,
            jnp.asarray(fwd).astype(BF16), jnp.asarray(inv).astype(BF16)]
    aliases = {}
    if prev_out is not None:
        in_specs.append(pl.BlockSpec(memory_space=pl.ANY))
        args.append(prev_out)
        aliases = {len(args) - 1: 0}

    def body(*refs):
        refs = list(refs)
        if prev_out is not None:
            del refs[len(args) - 1]
        _hy_kernel(*refs, seq=seq)

    return pl.pallas_call(
        body,
        grid=(n_g, n_seq),
        in_specs=in_specs,
        out_specs=pl.BlockSpec((seq, g), lambda c, b: (blk0 + b, c)),
        out_shape=jax.ShapeDtypeStruct((T_ALL, HY_WIDTH), F32),
        input_output_aliases=aliases,
        compiler_params=_params(("arbitrary", "arbitrary")),
        name="hyena_%d" % seq,
    )(*args)


def _softmax_av(s, v, sink):
    m = jnp.max(s, axis=-1, keepdims=True)
    if sink is not None:
        m = jnp.maximum(m, sink)
    e = jnp.exp(s - m)
    den = jnp.sum(e, axis=-1, keepdims=True)
    if sink is not None:
        den = den + jnp.exp(sink - m)
    return _dot(e, v) / den


def _attn_ctx_kernel(p_ref, qn_ref, kn_ref, sink_ref, o_ref, kc_ref, vc_ref, kd_ref, vd_ref):
    hd = HEAD_DIM
    scale = hd ** -0.5
    kv_w = KV_HEADS * hd
    q_w = KV_HEADS * Q_GROUP * hd
    vc_ref[0] = p_ref[:, q_w + kv_w:q_w + 2 * kv_w]
    base_d = q_w + 2 * kv_w
    kd_ref[0] = p_ref[:, base_d + q_w:base_d + q_w + kv_w]
    vd_ref[0] = p_ref[:, base_d + q_w + kv_w:base_d + q_w + 2 * kv_w]
    for kv in range(KV_HEADS):
        ks = slice(kv * hd, (kv + 1) * hd)
        k = _rms(p_ref[:, q_w + kv * hd:q_w + (kv + 1) * hd], kn_ref[...])
        kc_ref[0, :, ks] = k
        v = p_ref[:, q_w + kv_w + kv * hd:q_w + kv_w + (kv + 1) * hd]
        for g in range(Q_GROUP):
            h = kv * Q_GROUP + g
            q = _rms(p_ref[:, h * hd:(h + 1) * hd], qn_ref[...]) * scale
            o_ref[:, h * hd:(h + 1) * hd] = _softmax_av(_dot_nt(q, k), v, None)
        k = p_ref[:, base_d + q_w + kv * hd:base_d + q_w + (kv + 1) * hd]
        v = p_ref[:, base_d + q_w + kv_w + kv * hd:base_d + q_w + kv_w + (kv + 1) * hd]
        for g in range(Q_GROUP):
            h = kv * Q_GROUP + g
            q = p_ref[:, base_d + h * hd:base_d + (h + 1) * hd] * scale
            o_ref[:, q_w + h * hd:q_w + (h + 1) * hd] = _softmax_av(
                _dot_nt(q, k), v, sink_ref[0:1, h:h + 1])


def _attn_ctx(proj, q_norm, k_norm, sink):
    kv_w = KV_HEADS * HEAD_DIM
    cache = jax.ShapeDtypeStruct((BATCH, SEQ, kv_w), F32)
    cache_spec = pl.BlockSpec((1, SEQ, kv_w), lambda b: (b, 0, 0))
    return pl.pallas_call(
        _attn_ctx_kernel,
        grid=(BATCH,),
        in_specs=[
            pl.BlockSpec((SEQ, ODD_COLS), lambda b: (b, 0)),
            pl.BlockSpec((1, HEAD_DIM), lambda b: (0, 0)),
            pl.BlockSpec((1, HEAD_DIM), lambda b: (0, 0)),
            pl.BlockSpec((1, LANES), lambda b: (0, 0)),
        ],
        out_specs=[pl.BlockSpec((SEQ, D_MODEL), lambda b: (b, 0)),
                   cache_spec, cache_spec, cache_spec, cache_spec],
        out_shape=[jax.ShapeDtypeStruct((T_ALL, D_MODEL), F32), cache, cache, cache, cache],
        compiler_params=_params(("arbitrary",)),
        name="attn_ctx",
    )(proj, q_norm.reshape(1, -1), k_norm.reshape(1, -1), sink)


def _rope(x, cos, sin_signed):
    lane = lax.broadcasted_iota(jnp.int32, (1, HEAD_DIM), 1)
    lo = (lane % 64) < 32
    partner = jnp.where(lo, pltpu.roll(x, HEAD_DIM - 32, 1), pltpu.roll(x, 32, 1))
    return x * cos + partner * sin_signed


def _attn_lat_kernel(p_ref, ckc_ref, cvc_ref, ckd_ref, cvd_ref, qn_ref, kn_ref, sink_ref,
                     cos_ref, sin_ref, o_ref, k_s, v_s, q_s, *, q_blk):
    hd = HEAD_DIM
    scale = hd ** -0.5
    seq = DEC_SEQ
    kv_w = KV_HEADS * hd
    q_w = KV_HEADS * Q_GROUP * hd
    n_keys = PAST_LEN + seq
    n_blk = seq // q_blk
    cos = cos_ref[...]
    sin = sin_ref[...]
    kpos = lax.broadcasted_iota(jnp.int32, (1, n_keys), 1) - PAST_LEN
    qrow = lax.broadcasted_iota(jnp.int32, (Q_GROUP * q_blk, 1), 0) % q_blk
    for mixer in range(2):
        base = mixer * (q_w + 2 * kv_w)
        ck_ref, cv_ref = (ckc_ref, cvc_ref) if mixer == 0 else (ckd_ref, cvd_ref)
        for kv in range(KV_HEADS):
            ks = slice(kv * hd, (kv + 1) * hd)
            k = p_ref[:, base + q_w + kv * hd:base + q_w + (kv + 1) * hd]
            if mixer == 0:
                k = _rms(k, kn_ref[...])
            k_s[:PAST_LEN, :] = ck_ref[0, :, ks].astype(BF16)
            k_s[PAST_LEN:, :] = _rope(k, cos, sin).astype(BF16)
            v_s[:PAST_LEN, :] = cv_ref[0, :, ks].astype(BF16)
            v_s[PAST_LEN:, :] = p_ref[:, base + q_w + kv_w + kv * hd:
                                      base + q_w + kv_w + (kv + 1) * hd].astype(BF16)
            for g in range(Q_GROUP):
                h = kv * Q_GROUP + g
                q = p_ref[:, base + h * hd:base + (h + 1) * hd]
                if mixer == 0:
                    q = _rms(q, qn_ref[...])
                q = _rope(q, cos, sin) * scale
                for b in range(n_blk):
                    q_s[b, g * q_blk:(g + 1) * q_blk, :] = q[b * q_blk:(b + 1) * q_blk].astype(BF16)

            def blk(b, carry):
                s = lax.dot_general(q_s[b], k_s[...], (((1,), (1,)), ((), ())),
                                    preferred_element_type=F32)
                if mixer == 1:
                    qpos = b * q_blk + qrow
                    ok = (kpos < 0) | (jnp.abs(kpos - qpos) <= WINDOW)
                    s = jnp.where(ok, s, NEG_BIG)
                r0 = pl.multiple_of(b * q_blk, q_blk)
                for g in range(Q_GROUP):
                    h = kv * Q_GROUP + g
                    sink = sink_ref[0:1, h:h + 1] if mixer == 1 else None
                    o = _softmax_av(s[g * q_blk:(g + 1) * q_blk], v_s[...], sink)
                    o_ref[pl.ds(r0, q_blk), mixer * q_w + h * hd:mixer * q_w + (h + 1) * hd] = o
                return carry

            lax.fori_loop(0, n_blk, blk, 0)


def _attn_lat(proj, ck_c, cv_c, ck_d, cv_d, q_norm, k_norm, sink, cos, sin, prev_out):
    kv_w = KV_HEADS * HEAD_DIM
    q_blk = 256
    blk0 = T_CTX // DEC_SEQ
    cache_spec = pl.BlockSpec((1, PAST_LEN, kv_w), lambda b: (b, 0, 0))
    row = lambda n: pl.BlockSpec((1, n), lambda b: (0, 0))
    tab = pl.BlockSpec((DEC_SEQ, HEAD_DIM), lambda b: (0, 0))

    def body(*refs):
        refs = list(refs)
        del refs[10]
        _attn_lat_kernel(*refs, q_blk=q_blk)

    return pl.pallas_call(
        body,
        grid=(DEC_BATCH,),
        in_specs=[pl.BlockSpec((DEC_SEQ, ODD_COLS), lambda b: (blk0 + b, 0)),
                  cache_spec, cache_spec, cache_spec, cache_spec,
                  row(HEAD_DIM), row(HEAD_DIM), row(LANES), tab, tab,
                  pl.BlockSpec(memory_space=pl.ANY)],
        out_specs=pl.BlockSpec((DEC_SEQ, D_MODEL), lambda b: (blk0 + b, 0)),
        out_shape=jax.ShapeDtypeStruct((T_ALL, D_MODEL), F32),
        scratch_shapes=[pltpu.VMEM((PAST_LEN + DEC_SEQ, HEAD_DIM), BF16),
                        pltpu.VMEM((PAST_LEN + DEC_SEQ, HEAD_DIM), BF16),
                        pltpu.VMEM((DEC_SEQ // q_blk, Q_GROUP * q_blk, HEAD_DIM), BF16)],
        input_output_aliases={10: 0},
        compiler_params=_params(("arbitrary",)),
        name="attn_lat",
    )(proj, ck_c, cv_c, ck_d, cv_d, q_norm.reshape(1, -1), k_norm.reshape(1, -1), sink, cos, sin,
      prev_out)


def _rope_tables(seq):
    rows = seq // GRID_W
    row = jnp.repeat(jnp.arange(rows), GRID_W).astype(F32)
    col = (jnp.arange(rows * GRID_W) % GRID_W).astype(F32)
    half = HEAD_DIM // 2
    inv = ROPE_THETA ** (-jnp.arange(0, half, 2, dtype=F32) / half)
    ar = row[:, None] * inv
    ac = col[:, None] * inv
    cos = jnp.concatenate([jnp.cos(ar), jnp.cos(ar), jnp.cos(ac), jnp.cos(ac)], axis=-1)
    sin = jnp.concatenate([-jnp.sin(ar), jnp.sin(ar), -jnp.sin(ac), jnp.sin(ac)], axis=-1)
    return cos, sin


def _ffn_kernel(*refs, n_mix, tile, final):
    refs = list(refs)
    x_ref = refs[0]
    mix_refs = refs[1:1 + n_mix]
    wo_refs = refs[1 + n_mix:1 + 2 * n_mix]
    pos = 1 + 2 * n_mix
    mod_ref, gain_ref, wup_ref, cw_ref, cb_ref, wdn_ref = refs[pos:pos + 6]
    pos += 6
    fin_ref = None
    if final:
        fin_ref = refs[pos]
        pos += 1
    o_ref = refs[pos]
    h_s, acc_s = refs[pos + 1:]

    m = mod_ref[0]
    mixed = None
    for mr, wr in zip(mix_refs, wo_refs):
        part = jnp.dot(mr[...].astype(BF16), wr[...], preferred_element_type=F32)
        mixed = part if mixed is None else mixed + part
    x1 = x_ref[...] + m[2:3] * mixed
    o_ref[...] = x1
    h_s[...] = (_rms(x1, gain_ref[...]) * (1.0 + m[4:5]) + m[3:4]).astype(BF16)
    acc_s[...] = jnp.zeros_like(acc_s)

    seq_mask = jnp.where(pl.program_id(0) < T_CTX // tile, SEQ - 1, DEC_SEQ - 1)
    pos_in_seq = lax.broadcasted_iota(jnp.int32, (tile, 1), 0) & seq_mask
    first = pos_in_seq == 0
    last = pos_in_seq == seq_mask

    def chunk(c, carry):
        h = h_s[...]
        a = jnp.dot(h, wup_ref[c], preferred_element_type=F32)
        b = jnp.dot(h, wup_ref[N_FF_CHUNKS + c], preferred_element_type=F32)
        a = _conv3(a, cw_ref[c], first, last, tile) + cb_ref[c]
        hid = (_silu(a) * b).astype(BF16)
        acc_s[...] += jnp.dot(hid, wdn_ref[c], preferred_element_type=F32)
        return carry

    lax.fori_loop(0, N_FF_CHUNKS, chunk, 0)
    x2 = o_ref[...] + m[5:6] * acc_s[...]
    if final:
        x2 = _rms(x2, fin_ref[...])
    o_ref[...] = x2


def _ffn(x, mixes, w_outs, mod, gain, w_up, conv_w, conv_b, w_down, final_gain):
    tile = 1024
    n_mix = len(mixes)
    final = final_gain is not None
    const = lambda shape: pl.BlockSpec(shape, lambda i: (0,) * len(shape),
                                       pipeline_mode=pl.Buffered(1))
    in_specs = [pl.BlockSpec((tile, D_MODEL), lambda i: (i, 0))]
    in_specs += [pl.BlockSpec((tile, mx.shape[1]), lambda i: (i, 0)) for mx in mixes]
    in_specs += [const(w.shape) for w in w_outs]
    in_specs += [
        pl.BlockSpec((1, 6, D_MODEL), lambda i: (_mod_row(i, tile), 0, 0)),
        const((1, D_MODEL)),
        const((2 * N_FF_CHUNKS, D_MODEL, FF_CHUNK)),
        const((N_FF_CHUNKS, 3, FF_CHUNK)),
        const((N_FF_CHUNKS, 1, FF_CHUNK)),
        const((N_FF_CHUNKS, FF_CHUNK, D_MODEL)),
    ]
    args = [x, *mixes, *w_outs, mod, gain.reshape(1, D_MODEL), w_up, conv_w, conv_b, w_down]
    if final:
        in_specs.append(const((1, D_MODEL)))
        args.append(final_gain.reshape(1, D_MODEL))
    return pl.pallas_call(
        functools.partial(_ffn_kernel, n_mix=n_mix, tile=tile, final=final),
        grid=(T_ALL // tile,),
        in_specs=in_specs,
        out_specs=pl.BlockSpec((tile, D_MODEL), lambda i: (i, 0)),
        out_shape=jax.ShapeDtypeStruct((T_ALL, D_MODEL), F32),
        scratch_shapes=[pltpu.VMEM((tile, D_MODEL), BF16), pltpu.VMEM((tile, D_MODEL), F32)],
        compiler_params=_params(("arbitrary",)),
        name="outproj_ffn",
    )(*args)


def _even_weight(w):
    ha = MIX_HALF
    n_ba = 4 * DN_HEADS
    main = jnp.concatenate([w[:, :3 * ha], w[:, 4 * ha + n_ba:], w[:, 3 * ha:4 * ha]], axis=1)
    ba = jnp.pad(w[:, 4 * ha:4 * ha + n_ba], ((0, 0), (0, LANES - n_ba)))
    return jnp.concatenate([main, ba], axis=1).astype(BF16)


def _chunked(w, n):
    return w.reshape(w.shape[0], n, FF_CHUNK).transpose(1, 0, 2)


def kernel(x_prompt, x_sample, state_dn, cache_k_c, cache_v_c, cache_k_d, cache_v_d, c, c_ctx,
           final_norm, w_mod, b_mod, norm_mix, norm_ffn, w_out, ffn_w_up, ffn_conv_w, ffn_conv_b,
           ffn_w_down, ev_w_in, dn_conv_w, dn_a_log, dn_dt_bias, dn_norm, hy_conv_w, hy_conv_b,
           hy_w1, hy_b1, hy_freq1, hy_w2, hy_b2, hy_freq2, hy_w3, hy_bias, od_w_in, c_q_norm,
           c_k_norm, d_sink):
    x = jnp.concatenate([x_prompt.reshape(T_CTX, D_MODEL), x_sample.reshape(T_LAT, D_MODEL)], axis=0)
    cvec = jnp.concatenate([c_ctx[None, :], c, jnp.zeros((MOD_ROWS - 1 - DEC_BATCH, D_MODEL), F32)], axis=0)
    mods = _modulation(cvec, w_mod, b_mod).reshape(DEPTH, MOD_ROWS, 6, D_MODEL)
    cos, sin = _rope_tables(DEC_SEQ)
    kv_w = KV_HEADS * HEAD_DIM

    states, k_c, v_c, k_d, v_d = [], [], [], [], []
    for i in range(DEPTH):
        j = i // 2
        mod = mods[i]
        if i % 2 == 0:
            proj = _inproj(x, mod, norm_mix[i], _even_weight(ev_w_in[j]))
            par = jnp.zeros((8, LANES), F32)
            par = par.at[0, 8:8 + 2 * DN_HEADS].set(dn_a_log[j].reshape(-1))
            par = par.at[1, 8:8 + 2 * DN_HEADS].set(dn_dt_bias[j].reshape(-1))
            norm = dn_norm[j].reshape(1, HEAD_DIM)
            o_mix, s_fin = _deltanet(proj, dn_conv_w[j], par, norm, None, None,
                                     seq=SEQ, n_seq=BATCH, row0=0)
            (o_mix,) = _deltanet(proj, dn_conv_w[j], par, norm, state_dn[:, j], o_mix,
                                 seq=DEC_SEQ, n_seq=DEC_BATCH, row0=T_CTX)
            states.append(s_fin)
            filt = (hy_w1[j], hy_b1[j], hy_freq1[j], hy_w2[j], hy_b2[j], hy_freq2[j], hy_w3[j])
            y_mix = _hyena(proj, hy_conv_w[j], hy_conv_b[j], _hyena_filters(SEQ, *filt), hy_bias[j],
                           None, seq=SEQ, n_seq=BATCH, row0=0)
            y_mix = _hyena(proj, hy_conv_w[j], hy_conv_b[j], _hyena_filters(DEC_SEQ, *filt),
                           hy_bias[j], y_mix, seq=DEC_SEQ, n_seq=DEC_BATCH, row0=T_CTX)
            mixes = [o_mix, y_mix]
            wo = w_out[i].astype(BF16)
            w_outs = [wo[:MIX_HALF], wo[MIX_HALF:]]
        else:
            proj = _inproj(x, mod, norm_mix[i], od_w_in[j].astype(BF16))
            sink = jnp.pad(d_sink[j].reshape(1, -1), ((0, 0), (0, LANES - d_sink.shape[1])))
            mix, kc, vc, kd, vd = _attn_ctx(proj, c_q_norm[j], c_k_norm[j], sink)
            k_c.append(kc)
            v_c.append(vc)
            k_d.append(kd)
            v_d.append(vd)
            flat = lambda t: t[:, j].reshape(DEC_BATCH, PAST_LEN, kv_w)
            mix = _attn_lat(proj, flat(cache_k_c), flat(cache_v_c), flat(cache_k_d), flat(cache_v_d),
                            c_q_norm[j], c_k_norm[j], sink, cos, sin, mix)
            mixes = [mix]
            w_outs = [w_out[i].astype(BF16)]
        w_up = _chunked(ffn_w_up[i].astype(BF16), 2 * N_FF_CHUNKS)
        conv_w = _chunked(ffn_conv_w[i], N_FF_CHUNKS)
        conv_b = _chunked(ffn_conv_b[i].reshape(1, D_FF), N_FF_CHUNKS)
        w_down = ffn_w_down[i].astype(BF16).reshape(N_FF_CHUNKS, FF_CHUNK, D_MODEL)
        x = _ffn(x, mixes, w_outs, mod, norm_ffn[i], w_up, conv_w, conv_b, w_down,
                 final_norm if i == DEPTH - 1 else None)

    y_prompt = x[:T_CTX].reshape(BATCH, SEQ, D_MODEL)
    y_sample = x[T_CTX:].reshape(DEC_BATCH, DEC_SEQ, D_MODEL)
    heads = lambda ts: jnp.stack(ts, axis=1).reshape(BATCH, len(ts), SEQ, KV_HEADS, HEAD_DIM)
    return (y_prompt, y_sample, jnp.stack(states, axis=1), heads(k_c), heads(v_c), heads(k_d),
            heads(v_d))
```

```python
import functools
import math

import jax
import jax.numpy as jnp
import numpy as np
from jax import lax
from jax.experimental import pallas as pl
from jax.experimental.pallas import tpu as pltpu

F32 = jnp.float32
BF16 = jnp.bfloat16

D_MODEL = 1024
BATCH = 16
SEQ = 256
DEPTH = 4
DEC_BATCH = 8
DEC_SEQ = 1024
PAST_LEN = 256
GRID_W = 64
HEAD_DIM = 128
MIX_HALF = D_MODEL // 2
DN_HEADS = MIX_HALF // HEAD_DIM
DN_CHUNK = 64
HY_WIDTH = MIX_HALF
HY_BANDS = 16
HY_FFN = 64
HY_DECAY_PCT_SHORT = 0.3
HY_DECAY_PCT_LONG = 1.5
HY_TARGET = 1e-2
KV_HEADS = 2
Q_GROUP = 2
WINDOW = 128
ROPE_THETA = 10000.0
D_FF = 2816
EPS = 1e-6
NEG_BIG = -1e30

T_CTX = BATCH * SEQ
T_LAT = DEC_BATCH * DEC_SEQ
T_ALL = T_CTX + T_LAT
MOD_ROWS = 16
LANES = 128
VMEM_LIMIT = 56 * 1024 * 1024
FF_CHUNK = 256
N_FF_CHUNKS = D_FF // FF_CHUNK
HY_GROUP = 256
EVEN_COLS = 3 * MIX_HALF + 3 * HY_WIDTH + MIX_HALF + LANES
ODD_COLS = 2048

HIGHEST = lax.Precision.HIGHEST


def _params(sem):
    return pltpu.CompilerParams(dimension_semantics=sem, vmem_limit_bytes=VMEM_LIMIT)


def _sigmoid(x):
    return 1.0 / (1.0 + jnp.exp(-x))


def _silu(x):
    return x * _sigmoid(x)


def _softplus(x):
    return jnp.maximum(x, 0.0) + jnp.log1p(jnp.exp(-jnp.abs(x)))


def _dot(a, b):
    return jnp.dot(a.astype(BF16), b.astype(BF16), preferred_element_type=F32)


def _dot_nt(a, b):
    return lax.dot_general(a.astype(BF16), b.astype(BF16), (((1,), (1,)), ((), ())),
                           preferred_element_type=F32)


def _dot_tn(a, b):
    return lax.dot_general(a.astype(BF16), b.astype(BF16), (((0,), (0,)), ((), ())),
                           preferred_element_type=F32)


def _dot_split(a, b):
    a_hi = a.astype(BF16)
    b_hi = b.astype(BF16)
    a_lo = (a - a_hi.astype(F32)).astype(BF16)
    b_lo = (b - b_hi.astype(F32)).astype(BF16)
    dot = functools.partial(jnp.dot, preferred_element_type=F32)
    return dot(a_hi, b_hi) + dot(a_hi, b_lo) + dot(a_lo, b_hi)


def _dot_hi(a, b):
    return jnp.dot(a, b, preferred_element_type=F32, precision=HIGHEST)


def _rms(x, gain):
    return x * lax.rsqrt(jnp.mean(x * x, axis=-1, keepdims=True) + EPS) * gain


def _mod_row(i, tile):
    n_ctx = T_CTX // tile
    return jnp.where(i < n_ctx, 0, 1 + (i - n_ctx) // (DEC_SEQ // tile))


def _mod_kernel(c_ref, w_ref, b_ref, o_ref):
    o_ref[0] = _dot(_silu(c_ref[...]), w_ref[0]) + b_ref[0]


def _modulation(cvec, w_mod, b_mod):
    n_tile = 1536
    return pl.pallas_call(
        _mod_kernel,
        grid=(DEPTH, 6 * D_MODEL // n_tile),
        in_specs=[
            pl.BlockSpec((MOD_ROWS, D_MODEL), lambda l, n: (0, 0)),
            pl.BlockSpec((1, D_MODEL, n_tile), lambda l, n: (l, 0, n)),
            pl.BlockSpec((1, 1, n_tile), lambda l, n: (l, 0, n)),
        ],
        out_specs=pl.BlockSpec((1, MOD_ROWS, n_tile), lambda l, n: (l, 0, n)),
        out_shape=jax.ShapeDtypeStruct((DEPTH, MOD_ROWS, 6 * D_MODEL), F32),
        compiler_params=_params(("arbitrary", "arbitrary")),
        name="modulation",
    )(cvec, w_mod, b_mod.reshape(DEPTH, 1, 6 * D_MODEL))


def _inproj_kernel(x_ref, mod_ref, gain_ref, w_ref, o_ref, *, n_chunk):
    m = mod_ref[0]
    h = _rms(x_ref[...], gain_ref[...]) * (1.0 + m[1:2]) + m[0:1]
    hb = h.astype(BF16)
    n_cols = o_ref.shape[1]
    for c0 in range(0, n_cols, n_chunk):
        c1 = min(c0 + n_chunk, n_cols)
        o_ref[:, c0:c1] = jnp.dot(hb, w_ref[:, c0:c1], preferred_element_type=F32)


def _inproj(x, mod, gain, w):
    tile = 512
    n_cols = w.shape[1]
    return pl.pallas_call(
        functools.partial(_inproj_kernel, n_chunk=512),
        grid=(T_ALL // tile,),
        in_specs=[
            pl.BlockSpec((tile, D_MODEL), lambda i: (i, 0)),
            pl.BlockSpec((1, 6, D_MODEL), lambda i: (_mod_row(i, tile), 0, 0)),
            pl.BlockSpec((1, D_MODEL), lambda i: (0, 0)),
            pl.BlockSpec((D_MODEL, n_cols), lambda i: (0, 0)),
        ],
        out_specs=pl.BlockSpec((tile, n_cols), lambda i: (i, 0)),
        out_shape=jax.ShapeDtypeStruct((T_ALL, n_cols), F32),
        compiler_params=_params(("arbitrary",)),
        name="inproj",
    )(x, mod, gain.reshape(1, D_MODEL), w)


def _conv3(x, w, first, last, seq_rows):
    xp = jnp.where(first, 0.0, pltpu.roll(x, 1, 0))
    xn = jnp.where(last, 0.0, pltpu.roll(x, seq_rows - 1, 0))
    return xp * w[0:1] + x * w[1:2] + xn * w[2:3]


def _dn_kernel(*refs, seq, use_s0, write_state):
    refs = list(refs)
    q_ref, k_ref, v_ref, z_ref, ba_ref, cwq_ref, cwk_ref, cwv_ref, par_ref, norm_ref = refs[:10]
    pos = 10
    s0_ref = None
    if use_s0:
        s0_ref = refs[pos]
        pos += 1
    o_ref = refs[pos]
    pos += 1
    sfin_ref = None
    if write_state:
        sfin_ref = refs[pos]
        pos += 1
    q_s, k_s, v_s, of_s, ob_s, g_s, gr_s, beta_s, st_s = refs[pos:]

    C = DN_CHUNK
    n_chunks = seq // C
    hd = HEAD_DIM

    rows = lax.broadcasted_iota(jnp.int32, (seq, 1), 0)
    first = rows == 0
    last = rows == seq - 1
    for h in range(DN_HEADS):
        hs = slice(h * hd, (h + 1) * hd)
        q = _silu(_conv3(q_ref[:, hs], cwq_ref[:, hs], first, last, seq))
        k = _silu(_conv3(k_ref[:, hs], cwk_ref[:, hs], first, last, seq))
        v = _silu(_conv3(v_ref[:, hs], cwv_ref[:, hs], first, last, seq))
        q_s[:, hs] = q * lax.rsqrt(jnp.sum(q * q, axis=-1, keepdims=True) + EPS) * (hd ** -0.5)
        k_s[:, hs] = k * lax.rsqrt(jnp.sum(k * k, axis=-1, keepdims=True) + EPS)
        v_s[:, hs] = v

    ba = ba_ref[...]
    beta_s[...] = _sigmoid(ba)
    g = -jnp.exp(par_ref[0:1]) * _softplus(ba + par_ref[1:2])
    lane = lax.broadcasted_iota(jnp.int32, (1, LANES), 1)
    fwd_lane = lane < 8 + DN_HEADS
    ci = lax.broadcasted_iota(jnp.int32, (C, C), 0)
    cj = lax.broadcasted_iota(jnp.int32, (C, C), 1)
    tril1 = (ci >= cj).astype(F32)
    triu1 = (ci <= cj).astype(F32)
    for n in range(n_chunks):
        gn = g[n * C:(n + 1) * C]
        gc = jnp.where(fwd_lane, _dot_hi(tril1, gn), _dot_hi(triu1, gn))
        g_s[n * C:(n + 1) * C] = gc
        gr_s[n] = gc.T

    if use_s0:
        for d in range(2):
            for h in range(DN_HEADS):
                st_s[d * DN_HEADS + h] = s0_ref[0, d, h]
    else:
        st_s[...] = jnp.zeros_like(st_s)

    def step(t, carry):
        loaded = []
        for d in range(2):
            n = t if d == 0 else n_chunks - 1 - t
            r0 = pl.multiple_of(n * C, C)
            r_last = r0 + (C - 1 if d == 0 else 0)
            for h in range(DN_HEADS):
                p = d * DN_HEADS + h
                hs = slice(h * hd, (h + 1) * hd)
                loaded.append((
                    q_s[pl.ds(r0, C), hs], k_s[pl.ds(r0, C), hs], v_s[pl.ds(r0, C), hs],
                    g_s[pl.ds(r0, C), 8 + p:9 + p],
                    gr_s[n, 8 + p:9 + p, :],
                    beta_s[pl.ds(r0, C), p:p + 1],
                    g_s[pl.ds(r_last, 1), 8 + p:9 + p],
                    st_s[p], r0))
        n_p = len(loaded)
        qn, kn, vn, gcol, grow, bcol, glast, s, r0 = (list(col) for col in zip(*loaded))
        incl = [(ci >= cj) if p < DN_HEADS else (ci <= cj) for p in range(n_p)]
        strict = [(ci > cj) if p < DN_HEADS else (ci < cj) for p in range(n_p)]
        decay = [jnp.exp(jnp.where(incl[p], gcol[p] - grow[p], -jnp.inf)) for p in range(n_p)]
        kb = [kn[p] * bcol[p] for p in range(n_p)]
        y = [jnp.where(strict[p], -(_dot_nt(kb[p], kn[p]) * decay[p]), 0.0) for p in range(n_p)]
        attn = [_dot_nt(qn[p], kn[p]) * decay[p] for p in range(n_p)]
        q = [_dot_split(y[p], y[p]) for p in range(n_p)]
        for _ in range(4):
            both = [_dot_split(jnp.concatenate([q[p], y[p]], axis=0), q[p]) for p in range(n_p)]
            y = [y[p] + q[p] + both[p][C:] for p in range(n_p)]
            q = [both[p][:C] for p in range(n_p)]
        yq = [_dot_split(y[p], q[p]) for p in range(n_p)]
        y = [y[p] + q[p] + yq[p] for p in range(n_p)]
        egc = [jnp.exp(gcol[p]) for p in range(n_p)]
        rhs = [jnp.concatenate([vn[p] * bcol[p], kb[p] * egc[p]], axis=1) for p in range(n_p)]
        sol = [rhs[p] + _dot_split(y[p], rhs[p]) for p in range(n_p)]
        v_new = [sol[p][:, :hd] - _dot(sol[p][:, hd:], s[p]) for p in range(n_p)]
        o_old = [_dot(qn[p] * egc[p], s[p]) for p in range(n_p)]
        o = [o_old[p] + _dot(attn[p], v_new[p]) for p in range(n_p)]
        s_new = [s[p] * jnp.exp(glast[p]) + _dot_tn(kn[p] * jnp.exp(glast[p] - gcol[p]), v_new[p])
                 for p in range(n_p)]
        for p in range(n_p):
            d, h = divmod(p, DN_HEADS)
            st_s[p] = s_new[p]
            (of_s if d == 0 else ob_s)[pl.ds(r0[p], C), h * hd:(h + 1) * hd] = o[p]
        return carry

    lax.fori_loop(0, n_chunks, step, 0)

    for h in range(DN_HEADS):
        hs = slice(h * hd, (h + 1) * hd)
        o_ref[:, hs] = _rms(of_s[:, hs] + ob_s[:, hs], norm_ref[...]) * _silu(z_ref[:, hs])
    if write_state:
        for d in range(2):
            for h in range(DN_HEADS):
                sfin_ref[0, d, h] = st_s[d * DN_HEADS + h]


def _deltanet(proj, conv_w, par, norm, s0, prev_out, *, seq, n_seq, row0):
    use_s0 = s0 is not None
    write_state = not use_s0
    blk0 = row0 // seq
    half = MIX_HALF
    in_specs = [
        pl.BlockSpec((seq, half), lambda b: (blk0 + b, 0)),
        pl.BlockSpec((seq, half), lambda b: (blk0 + b, 1)),
        pl.BlockSpec((seq, half), lambda b: (blk0 + b, 2)),
        pl.BlockSpec((seq, half), lambda b: (blk0 + b, 6)),
        pl.BlockSpec((seq, LANES), lambda b: (blk0 + b, 28)),
        pl.BlockSpec((3, half), lambda b: (0, 0)),
        pl.BlockSpec((3, half), lambda b: (0, 1)),
        pl.BlockSpec((3, half), lambda b: (0, 2)),
        pl.BlockSpec((8, LANES), lambda b: (0, 0)),
        pl.BlockSpec((1, HEAD_DIM), lambda b: (0, 0)),
    ]
    args = [proj, proj, proj, proj, proj, conv_w, conv_w, conv_w, par, norm]
    if use_s0:
        in_specs.append(pl.BlockSpec((1, 2, DN_HEADS, HEAD_DIM, HEAD_DIM), lambda b: (b, 0, 0, 0, 0)))
        args.append(s0)
    aliases = {}
    if prev_out is not None:
        in_specs.append(pl.BlockSpec(memory_space=pl.ANY))
        args.append(prev_out)
        aliases = {len(args) - 1: 0}
    out_specs = [pl.BlockSpec((seq, half), lambda b: (blk0 + b, 0))]
    out_shape = [jax.ShapeDtypeStruct((T_ALL, half), F32)]
    if write_state:
        out_specs.append(pl.BlockSpec((1, 2, DN_HEADS, HEAD_DIM, HEAD_DIM), lambda b: (b, 0, 0, 0, 0)))
        out_shape.append(jax.ShapeDtypeStruct((n_seq, 2, DN_HEADS, HEAD_DIM, HEAD_DIM), F32))

    def body(*refs):
        refs = list(refs)
        if prev_out is not None:
            n_in = len(args)
            del refs[n_in - 1]
        _dn_kernel(*refs, seq=seq, use_s0=use_s0, write_state=write_state)

    n_chunks = seq // DN_CHUNK
    res = pl.pallas_call(
        body,
        grid=(n_seq,),
        in_specs=in_specs,
        out_specs=out_specs,
        out_shape=out_shape,
        scratch_shapes=[
            pltpu.VMEM((seq, half), F32),
            pltpu.VMEM((seq, half), F32),
            pltpu.VMEM((seq, half), F32),
            pltpu.VMEM((seq, half), F32),
            pltpu.VMEM((seq, half), F32),
            pltpu.VMEM((seq, LANES), F32),
            pltpu.VMEM((n_chunks, LANES, DN_CHUNK), F32),
            pltpu.VMEM((seq, LANES), F32),
            pltpu.VMEM((2 * DN_HEADS, HEAD_DIM, HEAD_DIM), F32),
        ],
        input_output_aliases=aliases,
        compiler_params=_params(("arbitrary",)),
        name="deltanet_%d" % seq,
    )(*args)
    return res


@functools.lru_cache(maxsize=None)
def _dft_tables(seq):
    k = np.arange(seq, dtype=np.int64)[:, None]
    s = np.arange(seq, dtype=np.int64)[None, :]
    ang = np.pi * ((k * s) % (2 * seq)).astype(np.float64) / seq
    cos = np.cos(ang)
    sin = np.sin(ang)
    fwd = np.concatenate([cos, -sin], axis=0)
    fwd[seq, :] = np.where(np.arange(seq) % 2 == 0, 1.0, -1.0)
    wk = np.full((seq,), 2.0)
    wk[0] = 1.0
    inv = np.concatenate([cos.T * wk[None, :], -2.0 * sin.T], axis=1) / (2.0 * seq)
    inv[:, seq] = np.where(np.arange(seq) % 2 == 0, 1.0, -1.0) / (2.0 * seq)
    return fwd.astype(np.float32), inv.astype(np.float32)


def _hy_filter_kernel(feat_ref, w1_ref, b1_ref, f1_ref, w2_ref, b2_ref, f2_ref, w3f_ref, w3b_ref,
                      dec_ref, fwd_ref, o_ref, *, seq):
    z = jnp.sin(f1_ref[...] * (_dot(feat_ref[...], w1_ref[...]) + b1_ref[...]))
    z = jnp.sin(f2_ref[...] * (_dot(z, w2_ref[...]) + b2_ref[...]))
    dec = dec_ref[...]
    h_fwd = _dot(z, w3f_ref[...]) * dec
    h_bwd = _dot(z, w3b_ref[...]) * dec
    rows = lax.broadcasted_iota(jnp.int32, (seq, 1), 0)
    h_bwd = jnp.where(rows == 0, 0.0, h_bwd)
    spec = _dot_hi(fwd_ref[...], jnp.concatenate([h_fwd, h_bwd], axis=1))
    g = h_fwd.shape[1]
    srow = lax.broadcasted_iota(jnp.int32, (2 * seq, 1), 0)
    sign = jnp.where(srow <= seq, 1.0, -1.0)
    o_ref[0] = spec[:, :g] + sign * spec[:, g:]


def _hyena_filters(seq, w1, b1, f1, w2, b2, f2, w3):
    t = jnp.linspace(0.0, 1.0, seq, dtype=F32)[:, None]
    w = (2.0 * math.pi / seq) * jnp.arange(seq, dtype=F32)[:, None]
    f = jnp.linspace(1e-4, HY_BANDS - 1, HY_BANDS, dtype=F32)[None, :]
    feats = jnp.concatenate([t, jnp.cos(f * w), -jnp.sin(f * w)], axis=-1)
    n_feat = feats.shape[1]
    feats = jnp.pad(feats, ((0, 0), (0, LANES - n_feat)))
    w1p = jnp.pad(w1, ((0, LANES - n_feat), (0, 0)))
    deltas = jnp.abs(jnp.linspace(math.log(HY_TARGET) / HY_DECAY_PCT_LONG,
                                  math.log(HY_TARGET) / HY_DECAY_PCT_SHORT, HY_WIDTH, dtype=F32))
    dec = jnp.exp(-t * deltas)
    fwd, _ = _dft_tables(seq)
    g = HY_GROUP
    n_g = HY_WIDTH // g
    small = lambda shape: pl.BlockSpec(shape, lambda o, c: (0,) * len(shape))
    return pl.pallas_call(
        functools.partial(_hy_filter_kernel, seq=seq),
        grid=(2, n_g),
        in_specs=[
            small((seq, LANES)), small((LANES, HY_FFN)), small((1, HY_FFN)), small((1, HY_FFN)),
            small((HY_FFN, HY_FFN)), small((1, HY_FFN)), small((1, HY_FFN)),
            pl.BlockSpec((HY_FFN, g), lambda o, c: (0, (2 * o) * n_g + c)),
            pl.BlockSpec((HY_FFN, g), lambda o, c: (0, (2 * o + 1) * n_g + c)),
            pl.BlockSpec((seq, g), lambda o, c: (0, c)),
            small((2 * seq, seq)),
        ],
        out_specs=pl.BlockSpec((1, 2 * seq, g), lambda o, c: (o, 0, c)),
        out_shape=jax.ShapeDtypeStruct((2, 2 * seq, HY_WIDTH), F32),
        compiler_params=_params(("arbitrary", "arbitrary")),
        name="hyena_filters_%d" % seq,
    )(feats, w1p, b1.reshape(1, -1), f1.reshape(1, -1), w2, b2.reshape(1, -1), f2.reshape(1, -1),
      w3, w3, dec, jnp.asarray(fwd))


def _hy_kernel(x1_ref, x2_ref, z_ref, cw1_ref, cw2_ref, cwz_ref, cb1_ref, cb2_ref, cbz_ref,
               spec_ref, bias_ref, fwd_ref, inv_ref, o_ref, *, seq):
    rows = lax.broadcasted_iota(jnp.int32, (seq, 1), 0)
    first = rows == 0
    last = rows == seq - 1
    gates = (_conv3(x1_ref[...], cw1_ref[...], first, last, seq) + cb1_ref[...],
             _conv3(x2_ref[...], cw2_ref[...], first, last, seq) + cb2_ref[...])
    z = _conv3(z_ref[...], cwz_ref[...], first, last, seq) + cbz_ref[...]
    for o in range(2):
        zf = jnp.dot(fwd_ref[...], z.astype(BF16), preferred_element_type=F32)
        re, im = zf[:seq], zf[seq:]
        hre, him = spec_ref[o, :seq, :], spec_ref[o, seq:, :]
        yre = re * hre - jnp.where(first, 0.0, im * him)
        yim = jnp.where(first, im * him, re * him + im * hre)
        y = jnp.dot(inv_ref[...], jnp.concatenate([yre, yim], axis=0).astype(BF16),
                    preferred_element_type=F32)
        z = gates[o] * (y + z * bias_ref[o:o + 1, :])
    o_ref[...] = z


def _hyena(proj, conv_w, conv_b, spec, bias, prev_out, *, seq, n_seq, row0):
    g = HY_GROUP
    n_g = HY_WIDTH // g
    blk0 = row0 // seq
    col0 = 3 * MIX_HALF // g
    fwd, inv = _dft_tables(seq)
    conv_b = conv_b.reshape(1, -1)

    def pcol(part):
        return pl.BlockSpec((seq, g), lambda c, b: (blk0 + b, col0 + part * n_g + c))

    def wcol(rows_, part):
        return pl.BlockSpec((rows_, g), lambda c, b: (0, part * n_g + c))

    in_specs = [pcol(0), pcol(1), pcol(2), wcol(3, 0), wcol(3, 1), wcol(3, 2),
                wcol(1, 0), wcol(1, 1), wcol(1, 2),
                pl.BlockSpec((2, 2 * seq, g), lambda c, b: (0, 0, c)),
                pl.BlockSpec((2, g), lambda c, b: (0, c)),
                pl.BlockSpec((2 * seq, seq), lambda c, b: (0, 0)),
                pl.BlockSpec((seq, 2 * seq), lambda c, b: (0, 0))]
    args = [proj, proj, proj, conv_w, conv_w, conv_w, conv_b, conv_b, conv_b, spec, bias,
            jnp.asarray(fwd).astype(BF16), jnp.asarray(inv).astype(BF16)]
    aliases = {}
    if prev_out is not None:
        in_specs.append(pl.BlockSpec(memory_space=pl.ANY))
        args.append(prev_out)
        aliases = {len(args) - 1: 0}

    def body(*refs):
        refs = list(refs)
        if prev_out is not None:
            del refs[len(args) - 1]
        _hy_kernel(*refs, seq=seq)

    return pl.pallas_call(
        body,
        grid=(n_g, n_seq),
        in_specs=in_specs,
        out_specs=pl.BlockSpec((seq, g), lambda c, b: (blk0 + b, c)),
        out_shape=jax.ShapeDtypeStruct((T_ALL, HY_WIDTH), F32),
        input_output_aliases=aliases,
        compiler_params=_params(("arbitrary", "arbitrary")),
        name="hyena_%d" % seq,
    )(*args)


def _softmax_av(s, v, sink):
    m = jnp.max(s, axis=-1, keepdims=True)
    if sink is not None:
        m = jnp.maximum(m, sink)
    e = jnp.exp(s - m)
    den = jnp.sum(e, axis=-1, keepdims=True)
    if sink is not None:
        den = den + jnp.exp(sink - m)
    return _dot(e, v) / den


def _attn_ctx_kernel(p_ref, qn_ref, kn_ref, sink_ref, o_ref, kc_ref, vc_ref, kd_ref, vd_ref):
    hd = HEAD_DIM
    scale = hd ** -0.5
    kv_w = KV_HEADS * hd
    q_w = KV_HEADS * Q_GROUP * hd
    vc_ref[0] = p_ref[:, q_w + kv_w:q_w + 2 * kv_w]
    base_d = q_w + 2 * kv_w
    kd_ref[0] = p_ref[:, base_d + q_w:base_d + q_w + kv_w]
    vd_ref[0] = p_ref[:, base_d + q_w + kv_w:base_d + q_w + 2 * kv_w]
    for kv in range(KV_HEADS):
        ks = slice(kv * hd, (kv + 1) * hd)
        k = _rms(p_ref[:, q_w + kv * hd:q_w + (kv + 1) * hd], kn_ref[...])
        kc_ref[0, :, ks] = k
        v = p_ref[:, q_w + kv_w + kv * hd:q_w + kv_w + (kv + 1) * hd]
        for g in range(Q_GROUP):
            h = kv * Q_GROUP + g
            q = _rms(p_ref[:, h * hd:(h + 1) * hd], qn_ref[...]) * scale
            o_ref[:, h * hd:(h + 1) * hd] = _softmax_av(_dot_nt(q, k), v, None)
        k = p_ref[:, base_d + q_w + kv * hd:base_d + q_w + (kv + 1) * hd]
        v = p_ref[:, base_d + q_w + kv_w + kv * hd:base_d + q_w + kv_w + (kv + 1) * hd]
        for g in range(Q_GROUP):
            h = kv * Q_GROUP + g
            q = p_ref[:, base_d + h * hd:base_d + (h + 1) * hd] * scale
            o_ref[:, q_w + h * hd:q_w + (h + 1) * hd] = _softmax_av(
                _dot_nt(q, k), v, sink_ref[0:1, h:h + 1])


def _attn_ctx(proj, q_norm, k_norm, sink):
    kv_w = KV_HEADS * HEAD_DIM
    cache = jax.ShapeDtypeStruct((BATCH, SEQ, kv_w), F32)
    cache_spec = pl.BlockSpec((1, SEQ, kv_w), lambda b: (b, 0, 0))
    return pl.pallas_call(
        _attn_ctx_kernel,
        grid=(BATCH,),
        in_specs=[
            pl.BlockSpec((SEQ, ODD_COLS), lambda b: (b, 0)),
            pl.BlockSpec((1, HEAD_DIM), lambda b: (0, 0)),
            pl.BlockSpec((1, HEAD_DIM), lambda b: (0, 0)),
            pl.BlockSpec((1, LANES), lambda b: (0, 0)),
        ],
        out_specs=[pl.BlockSpec((SEQ, D_MODEL), lambda b: (b, 0)),
                   cache_spec, cache_spec, cache_spec, cache_spec],
        out_shape=[jax.ShapeDtypeStruct((T_ALL, D_MODEL), F32), cache, cache, cache, cache],
        compiler_params=_params(("arbitrary",)),
        name="attn_ctx",
    )(proj, q_norm.reshape(1, -1), k_norm.reshape(1, -1), sink)


def _rope(x, cos, sin_signed):
    lane = lax.broadcasted_iota(jnp.int32, (1, HEAD_DIM), 1)
    lo = (lane % 64) < 32
    partner = jnp.where(lo, pltpu.roll(x, HEAD_DIM - 32, 1), pltpu.roll(x, 32, 1))
    return x * cos + partner * sin_signed


def _attn_lat_kernel(p_ref, ckc_ref, cvc_ref, ckd_ref, cvd_ref, qn_ref, kn_ref, sink_ref,
                     cos_ref, sin_ref, o_ref, k_s, v_s, q_s, *, q_blk):
    hd = HEAD_DIM
    scale = hd ** -0.5
    seq = DEC_SEQ
    kv_w = KV_HEADS * hd
    q_w = KV_HEADS * Q_GROUP * hd
    n_keys = PAST_LEN + seq
    n_blk = seq // q_blk
    cos = cos_ref[...]
    sin = sin_ref[...]
    kpos = lax.broadcasted_iota(jnp.int32, (1, n_keys), 1) - PAST_LEN
    qrow = lax.broadcasted_iota(jnp.int32, (Q_GROUP * q_blk, 1), 0) % q_blk
    for mixer in range(2):
        base = mixer * (q_w + 2 * kv_w)
        ck_ref, cv_ref = (ckc_ref, cvc_ref) if mixer == 0 else (ckd_ref, cvd_ref)
        for kv in range(KV_HEADS):
            ks = slice(kv * hd, (kv + 1) * hd)
            k = p_ref[:, base + q_w + kv * hd:base + q_w + (kv + 1) * hd]
            if mixer == 0:
                k = _rms(k, kn_ref[...])
            k_s[:PAST_LEN, :] = ck_ref[0, :, ks].astype(BF16)
            k_s[PAST_LEN:, :] = _rope(k, cos, sin).astype(BF16)
            v_s[:PAST_LEN, :] = cv_ref[0, :, ks].astype(BF16)
            v_s[PAST_LEN:, :] = p_ref[:, base + q_w + kv_w + kv * hd:
                                      base + q_w + kv_w + (kv + 1) * hd].astype(BF16)
            for g in range(Q_GROUP):
                h = kv * Q_GROUP + g
                q = p_ref[:, base + h * hd:base + (h + 1) * hd]
                if mixer == 0:
                    q = _rms(q, qn_ref[...])
                q = _rope(q, cos, sin) * scale
                for b in range(n_blk):
                    q_s[b, g * q_blk:(g + 1) * q_blk, :] = q[b * q_blk:(b + 1) * q_blk].astype(BF16)

            def blk(b, carry):
                s = lax.dot_general(q_s[b], k_s[...], (((1,), (1,)), ((), ())),
                                    preferred_element_type=F32)
                if mixer == 1:
                    qpos = b * q_blk + qrow
                    ok = (kpos < 0) | (jnp.abs(kpos - qpos) <= WINDOW)
                    s = jnp.where(ok, s, NEG_BIG)
                r0 = pl.multiple_of(b * q_blk, q_blk)
                for g in range(Q_GROUP):
                    h = kv * Q_GROUP + g
                    sink = sink_ref[0:1, h:h + 1] if mixer == 1 else None
                    o = _softmax_av(s[g * q_blk:(g + 1) * q_blk], v_s[...], sink)
                    o_ref[pl.ds(r0, q_blk), mixer * q_w + h * hd:mixer * q_w + (h + 1) * hd] = o
                return carry

            lax.fori_loop(0, n_blk, blk, 0)


def _attn_lat(proj, ck_c, cv_c, ck_d, cv_d, q_norm, k_norm, sink, cos, sin, prev_out):
    kv_w = KV_HEADS * HEAD_DIM
    q_blk = 256
    blk0 = T_CTX // DEC_SEQ
    cache_spec = pl.BlockSpec((1, PAST_LEN, kv_w), lambda b: (b, 0, 0))
    row = lambda n: pl.BlockSpec((1, n), lambda b: (0, 0))
    tab = pl.BlockSpec((DEC_SEQ, HEAD_DIM), lambda b: (0, 0))

    def body(*refs):
        refs = list(refs)
        del refs[10]
        _attn_lat_kernel(*refs, q_blk=q_blk)

    return pl.pallas_call(
        body,
        grid=(DEC_BATCH,),
        in_specs=[pl.BlockSpec((DEC_SEQ, ODD_COLS), lambda b: (blk0 + b, 0)),
                  cache_spec, cache_spec, cache_spec, cache_spec,
                  row(HEAD_DIM), row(HEAD_DIM), row(LANES), tab, tab,
                  pl.BlockSpec(memory_space=pl.ANY)],
        out_specs=pl.BlockSpec((DEC_SEQ, D_MODEL), lambda b: (blk0 + b, 0)),
        out_shape=jax.ShapeDtypeStruct((T_ALL, D_MODEL), F32),
        scratch_shapes=[pltpu.VMEM((PAST_LEN + DEC_SEQ, HEAD_DIM), BF16),
                        pltpu.VMEM((PAST_LEN + DEC_SEQ, HEAD_DIM), BF16),
                        pltpu.VMEM((DEC_SEQ // q_blk, Q_GROUP * q_blk, HEAD_DIM), BF16)],
        input_output_aliases={10: 0},
        compiler_params=_params(("arbitrary",)),
        name="attn_lat",
    )(proj, ck_c, cv_c, ck_d, cv_d, q_norm.reshape(1, -1), k_norm.reshape(1, -1), sink, cos, sin,
      prev_out)


def _rope_tables(seq):
    rows = seq // GRID_W
    row = jnp.repeat(jnp.arange(rows), GRID_W).astype(F32)
    col = (jnp.arange(rows * GRID_W) % GRID_W).astype(F32)
    half = HEAD_DIM // 2
    inv = ROPE_THETA ** (-jnp.arange(0, half, 2, dtype=F32) / half)
    ar = row[:, None] * inv
    ac = col[:, None] * inv
    cos = jnp.concatenate([jnp.cos(ar), jnp.cos(ar), jnp.cos(ac), jnp.cos(ac)], axis=-1)
    sin = jnp.concatenate([-jnp.sin(ar), jnp.sin(ar), -jnp.sin(ac), jnp.sin(ac)], axis=-1)
    return cos, sin


def _ffn_kernel(*refs, n_mix, tile, final):
    refs = list(refs)
    x_ref = refs[0]
    mix_refs = refs[1:1 + n_mix]
    wo_refs = refs[1 + n_mix:1 + 2 * n_mix]
    pos = 1 + 2 * n_mix
    mod_ref, gain_ref, wup_ref, cw_ref, cb_ref, wdn_ref = refs[pos:pos + 6]
    pos += 6
    fin_ref = None
    if final:
        fin_ref = refs[pos]
        pos += 1
    o_ref = refs[pos]
    h_s, acc_s = refs[pos + 1:]

    m = mod_ref[0]
    mixed = None
    for mr, wr in zip(mix_refs, wo_refs):
        part = jnp.dot(mr[...].astype(BF16), wr[...], preferred_element_type=F32)
        mixed = part if mixed is None else mixed + part
    x1 = x_ref[...] + m[2:3] * mixed
    o_ref[...] = x1
    h_s[...] = (_rms(x1, gain_ref[...]) * (1.0 + m[4:5]) + m[3:4]).astype(BF16)
    acc_s[...] = jnp.zeros_like(acc_s)

    seq_mask = jnp.where(pl.program_id(0) < T_CTX // tile, SEQ - 1, DEC_SEQ - 1)
    pos_in_seq = lax.broadcasted_iota(jnp.int32, (tile, 1), 0) & seq_mask
    first = pos_in_seq == 0
    last = pos_in_seq == seq_mask

    def chunk(c, carry):
        h = h_s[...]
        a = jnp.dot(h, wup_ref[c], preferred_element_type=F32)
        b = jnp.dot(h, wup_ref[N_FF_CHUNKS + c], preferred_element_type=F32)
        a = _conv3(a, cw_ref[c], first, last, tile) + cb_ref[c]
        hid = (_silu(a) * b).astype(BF16)
        acc_s[...] += jnp.dot(hid, wdn_ref[c], preferred_element_type=F32)
        return carry

    lax.fori_loop(0, N_FF_CHUNKS, chunk, 0)
    x2 = o_ref[...] + m[5:6] * acc_s[...]
    if final:
        x2 = _rms(x2, fin_ref[...])
    o_ref[...] = x2


def _ffn(x, mixes, w_outs, mod, gain, w_up, conv_w, conv_b, w_down, final_gain):
    tile = 1024
    n_mix = len(mixes)
    final = final_gain is not None
    const = lambda shape: pl.BlockSpec(shape, lambda i: (0,) * len(shape),
                                       pipeline_mode=pl.Buffered(1))
    in_specs = [pl.BlockSpec((tile, D_MODEL), lambda i: (i, 0))]
    in_specs += [pl.BlockSpec((tile, mx.shape[1]), lambda i: (i, 0)) for mx in mixes]
    in_specs += [const(w.shape) for w in w_outs]
    in_specs += [
        pl.BlockSpec((1, 6, D_MODEL), lambda i: (_mod_row(i, tile), 0, 0)),
        const((1, D_MODEL)),
        const((2 * N_FF_CHUNKS, D_MODEL, FF_CHUNK)),
        const((N_FF_CHUNKS, 3, FF_CHUNK)),
        const((N_FF_CHUNKS, 1, FF_CHUNK)),
        const((N_FF_CHUNKS, FF_CHUNK, D_MODEL)),
    ]
    args = [x, *mixes, *w_outs, mod, gain.reshape(1, D_MODEL), w_up, conv_w, conv_b, w_down]
    if final:
        in_specs.append(const((1, D_MODEL)))
        args.append(final_gain.reshape(1, D_MODEL))
    return pl.pallas_call(
        functools.partial(_ffn_kernel, n_mix=n_mix, tile=tile, final=final),
        grid=(T_ALL // tile,),
        in_specs=in_specs,
        out_specs=pl.BlockSpec((tile, D_MODEL), lambda i: (i, 0)),
        out_shape=jax.ShapeDtypeStruct((T_ALL, D_MODEL), F32),
        scratch_shapes=[pltpu.VMEM((tile, D_MODEL), BF16), pltpu.VMEM((tile, D_MODEL), F32)],
        compiler_params=_params(("arbitrary",)),
        name="outproj_ffn",
    )(*args)


def _even_weight(w):
    ha = MIX_HALF
    n_ba = 4 * DN_HEADS
    main = jnp.concatenate([w[:, :3 * ha], w[:, 4 * ha + n_ba:], w[:, 3 * ha:4 * ha]], axis=1)
    ba = jnp.pad(w[:, 4 * ha:4 * ha + n_ba], ((0, 0), (0, LANES - n_ba)))
    return jnp.concatenate([main, ba], axis=1).astype(BF16)


def _chunked(w, n):
    return w.reshape(w.shape[0], n, FF_CHUNK).transpose(1, 0, 2)


def kernel(x_prompt, x_sample, state_dn, cache_k_c, cache_v_c, cache_k_d, cache_v_d, c, c_ctx,
           final_norm, w_mod, b_mod, norm_mix, norm_ffn, w_out, ffn_w_up, ffn_conv_w, ffn_conv_b,
           ffn_w_down, ev_w_in, dn_conv_w, dn_a_log, dn_dt_bias, dn_norm, hy_conv_w, hy_conv_b,
           hy_w1, hy_b1, hy_freq1, hy_w2, hy_b2, hy_freq2, hy_w3, hy_bias, od_w_in, c_q_norm,
           c_k_norm, d_sink):
    x = jnp.concatenate([x_prompt.reshape(T_CTX, D_MODEL), x_sample.reshape(T_LAT, D_MODEL)], axis=0)
    cvec = jnp.concatenate([c_ctx[None, :], c, jnp.zeros((MOD_ROWS - 1 - DEC_BATCH, D_MODEL), F32)], axis=0)
    mods = _modulation(cvec, w_mod, b_mod).reshape(DEPTH, MOD_ROWS, 6, D_MODEL)
    cos, sin = _rope_tables(DEC_SEQ)
    kv_w = KV_HEADS * HEAD_DIM

    states, k_c, v_c, k_d, v_d = [], [], [], [], []
    for i in range(DEPTH):
        j = i // 2
        mod = mods[i]
        if i % 2 == 0:
            proj = _inproj(x, mod, norm_mix[i], _even_weight(ev_w_in[j]))
            par = jnp.zeros((8, LANES), F32)
            par = par.at[0, 8:8 + 2 * DN_HEADS].set(dn_a_log[j].reshape(-1))
            par = par.at[1, 8:8 + 2 * DN_HEADS].set(dn_dt_bias[j].reshape(-1))
            norm = dn_norm[j].reshape(1, HEAD_DIM)
            o_mix, s_fin = _deltanet(proj, dn_conv_w[j], par, norm, None, None,
                                     seq=SEQ, n_seq=BATCH, row0=0)
            (o_mix,) = _deltanet(proj, dn_conv_w[j], par, norm, state_dn[:, j], o_mix,
                                 seq=DEC_SEQ, n_seq=DEC_BATCH, row0=T_CTX)
            states.append(s_fin)
            filt = (hy_w1[j], hy_b1[j], hy_freq1[j], hy_w2[j], hy_b2[j], hy_freq2[j], hy_w3[j])
            y_mix = _hyena(proj, hy_conv_w[j], hy_conv_b[j], _hyena_filters(SEQ, *filt), hy_bias[j],
                           None, seq=SEQ, n_seq=BATCH, row0=0)
            y_mix = _hyena(proj, hy_conv_w[j], hy_conv_b[j], _hyena_filters(DEC_SEQ, *filt),
                           hy_bias[j], y_mix, seq=DEC_SEQ, n_seq=DEC_BATCH, row0=T_CTX)
            mixes = [o_mix, y_mix]
            wo = w_out[i].astype(BF16)
            w_outs = [wo[:MIX_HALF], wo[MIX_HALF:]]
        else:
            proj = _inproj(x, mod, norm_mix[i], od_w_in[j].astype(BF16))
            sink = jnp.pad(d_sink[j].reshape(1, -1), ((0, 0), (0, LANES - d_sink.shape[1])))
            mix, kc, vc, kd, vd = _attn_ctx(proj, c_q_norm[j], c_k_norm[j], sink)
            k_c.append(kc)
            v_c.append(vc)
            k_d.append(kd)
            v_d.append(vd)
            flat = lambda t: t[:, j].reshape(DEC_BATCH, PAST_LEN, kv_w)
            mix = _attn_lat(proj, flat(cache_k_c), flat(cache_v_c), flat(cache_k_d), flat(cache_v_d),
                            c_q_norm[j], c_k_norm[j], sink, cos, sin, mix)
            mixes = [mix]
            w_outs = [w_out[i].astype(BF16)]
        w_up = _chunked(ffn_w_up[i].astype(BF16), 2 * N_FF_CHUNKS)
        conv_w = _chunked(ffn_conv_w[i], N_FF_CHUNKS)
        conv_b = _chunked(ffn_conv_b[i].reshape(1, D_FF), N_FF_CHUNKS)
        w_down = ffn_w_down[i].astype(BF16).reshape(N_FF_CHUNKS, FF_CHUNK, D_MODEL)
        x = _ffn(x, mixes, w_outs, mod, norm_ffn[i], w_up, conv_w, conv_b, w_down,
                 final_norm if i == DEPTH - 1 else None)

    y_prompt = x[:T_CTX].reshape(BATCH, SEQ, D_MODEL)
    y_sample = x[T_CTX:].reshape(DEC_BATCH, DEC_SEQ, D_MODEL)
    heads = lambda ts: jnp.stack(ts, axis=1).reshape(BATCH, len(ts), SEQ, KV_HEADS, HEAD_DIM)
    return (y_prompt, y_sample, jnp.stack(states, axis=1), heads(k_c), heads(v_c), heads(k_d),
            heads(v_d))
```

```python
import functools
import math

import jax
import jax.numpy as jnp
import numpy as np
from jax import lax
from jax.experimental import pallas as pl
from jax.experimental.pallas import tpu as pltpu

F32 = jnp.float32
BF16 = jnp.bfloat16

D_MODEL = 1024
BATCH = 16
SEQ = 256
DEPTH = 4
DEC_BATCH = 8
DEC_SEQ = 1024
PAST_LEN = 256
GRID_W = 64
HEAD_DIM = 128
MIX_HALF = D_MODEL // 2
DN_HEADS = MIX_HALF // HEAD_DIM
DN_CHUNK = 64
HY_WIDTH = MIX_HALF
HY_BANDS = 16
HY_FFN = 64
HY_DECAY_PCT_SHORT = 0.3
HY_DECAY_PCT_LONG = 1.5
HY_TARGET = 1e-2
KV_HEADS = 2
Q_GROUP = 2
WINDOW = 128
ROPE_THETA = 10000.0
D_FF = 2816
EPS = 1e-6
NEG_BIG = -1e30

T_CTX = BATCH * SEQ
T_LAT = DEC_BATCH * DEC_SEQ
T_ALL = T_CTX + T_LAT
MOD_ROWS = 16
LANES = 128
VMEM_LIMIT = 56 * 1024 * 1024
FF_CHUNK = 256
N_FF_CHUNKS = D_FF // FF_CHUNK
HY_GROUP = 256
EVEN_COLS = 3 * MIX_HALF + 3 * HY_WIDTH + MIX_HALF + LANES
ODD_COLS = 2048

HIGHEST = lax.Precision.HIGHEST


def _params(sem):
    return pltpu.CompilerParams(dimension_semantics=sem, vmem_limit_bytes=VMEM_LIMIT)


def _sigmoid(x):
    return 1.0 / (1.0 + jnp.exp(-x))


def _silu(x):
    return x * _sigmoid(x)


def _softplus(x):
    return jnp.maximum(x, 0.0) + jnp.log1p(jnp.exp(-jnp.abs(x)))


def _dot(a, b):
    return jnp.dot(a.astype(BF16), b.astype(BF16), preferred_element_type=F32)


def _dot_nt(a, b):
    return lax.dot_general(a.astype(BF16), b.astype(BF16), (((1,), (1,)), ((), ())),
                           preferred_element_type=F32)


def _dot_tn(a, b):
    return lax.dot_general(a.astype(BF16), b.astype(BF16), (((0,), (0,)), ((), ())),
                           preferred_element_type=F32)


def _dot_split(a, b):
    a_hi = a.astype(BF16)
    b_hi = b.astype(BF16)
    a_lo = (a - a_hi.astype(F32)).astype(BF16)
    b_lo = (b - b_hi.astype(F32)).astype(BF16)
    dot = functools.partial(jnp.dot, preferred_element_type=F32)
    return dot(a_hi, b_hi) + dot(a_hi, b_lo) + dot(a_lo, b_hi)


def _dot_hi(a, b):
    return jnp.dot(a, b, preferred_element_type=F32, precision=HIGHEST)


def _rms(x, gain):
    return x * lax.rsqrt(jnp.mean(x * x, axis=-1, keepdims=True) + EPS) * gain


def _mod_row(i, tile):
    n_ctx = T_CTX // tile
    return jnp.where(i < n_ctx, 0, 1 + (i - n_ctx) // (DEC_SEQ // tile))


def _mod_kernel(c_ref, w_ref, b_ref, o_ref):
    o_ref[0] = _dot(_silu(c_ref[...]), w_ref[0]) + b_ref[0]


def _modulation(cvec, w_mod, b_mod):
    n_tile = 1536
    return pl.pallas_call(
        _mod_kernel,
        grid=(DEPTH, 6 * D_MODEL // n_tile),
        in_specs=[
            pl.BlockSpec((MOD_ROWS, D_MODEL), lambda l, n: (0, 0)),
            pl.BlockSpec((1, D_MODEL, n_tile), lambda l, n: (l, 0, n)),
            pl.BlockSpec((1, 1, n_tile), lambda l, n: (l, 0, n)),
        ],
        out_specs=pl.BlockSpec((1, MOD_ROWS, n_tile), lambda l, n: (l, 0, n)),
        out_shape=jax.ShapeDtypeStruct((DEPTH, MOD_ROWS, 6 * D_MODEL), F32),
        compiler_params=_params(("arbitrary", "arbitrary")),
        name="modulation",
    )(cvec, w_mod, b_mod.reshape(DEPTH, 1, 6 * D_MODEL))


def _inproj_kernel(x_ref, mod_ref, gain_ref, w_ref, o_ref, *, n_chunk):
    m = mod_ref[0]
    h = _rms(x_ref[...], gain_ref[...]) * (1.0 + m[1:2]) + m[0:1]
    hb = h.astype(BF16)
    n_cols = o_ref.shape[1]
    for c0 in range(0, n_cols, n_chunk):
        c1 = min(c0 + n_chunk, n_cols)
        o_ref[:, c0:c1] = jnp.dot(hb, w_ref[:, c0:c1], preferred_element_type=F32)


def _inproj(x, mod, gain, w):
    tile = 512
    n_cols = w.shape[1]
    return pl.pallas_call(
        functools.partial(_inproj_kernel, n_chunk=512),
        grid=(T_ALL // tile,),
        in_specs=[
            pl.BlockSpec((tile, D_MODEL), lambda i: (i, 0)),
            pl.BlockSpec((1, 6, D_MODEL), lambda i: (_mod_row(i, tile), 0, 0)),
            pl.BlockSpec((1, D_MODEL), lambda i: (0, 0)),
            pl.BlockSpec((D_MODEL, n_cols), lambda i: (0, 0)),
        ],
        out_specs=pl.BlockSpec((tile, n_cols), lambda i: (i, 0)),
        out_shape=jax.ShapeDtypeStruct((T_ALL, n_cols), F32),
        compiler_params=_params(("arbitrary",)),
        name="inproj",
    )(x, mod, gain.reshape(1, D_MODEL), w)


def _conv3(x, w, first, last, seq_rows):
    xp = jnp.where(first, 0.0, pltpu.roll(x, 1, 0))
    xn = jnp.where(last, 0.0, pltpu.roll(x, seq_rows - 1, 0))
    return xp * w[0:1] + x * w[1:2] + xn * w[2:3]


def _dn_kernel(*refs, seq, use_s0, write_state):
    refs = list(refs)
    q_ref, k_ref, v_ref, z_ref, ba_ref, cwq_ref, cwk_ref, cwv_ref, par_ref, norm_ref = refs[:10]
    pos = 10
    s0_ref = None
    if use_s0:
        s0_ref = refs[pos]
        pos += 1
    o_ref = refs[pos]
    pos += 1
    sfin_ref = None
    if write_state:
        sfin_ref = refs[pos]
        pos += 1
    q_s, k_s, v_s, of_s, ob_s, g_s, gr_s, beta_s, st_s = refs[pos:]

    C = DN_CHUNK
    n_chunks = seq // C
    hd = HEAD_DIM

    rows = lax.broadcasted_iota(jnp.int32, (seq, 1), 0)
    first = rows == 0
    last = rows == seq - 1
    for h in range(DN_HEADS):
        hs = slice(h * hd, (h + 1) * hd)
        q = _silu(_conv3(q_ref[:, hs], cwq_ref[:, hs], first, last, seq))
        k = _silu(_conv3(k_ref[:, hs], cwk_ref[:, hs], first, last, seq))
        v = _silu(_conv3(v_ref[:, hs], cwv_ref[:, hs], first, last, seq))
        q_s[:, hs] = q * lax.rsqrt(jnp.sum(q * q, axis=-1, keepdims=True) + EPS) * (hd ** -0.5)
        k_s[:, hs] = k * lax.rsqrt(jnp.sum(k * k, axis=-1, keepdims=True) + EPS)
        v_s[:, hs] = v

    ba = ba_ref[...]
    beta_s[...] = _sigmoid(ba)
    g = -jnp.exp(par_ref[0:1]) * _softplus(ba + par_ref[1:2])
    lane = lax.broadcasted_iota(jnp.int32, (1, LANES), 1)
    fwd_lane = lane < 8 + DN_HEADS
    ci = lax.broadcasted_iota(jnp.int32, (C, C), 0)
    cj = lax.broadcasted_iota(jnp.int32, (C, C), 1)
    tril1 = (ci >= cj).astype(F32)
    triu1 = (ci <= cj).astype(F32)
    for n in range(n_chunks):
        gn = g[n * C:(n + 1) * C]
        gc = jnp.where(fwd_lane, _dot_hi(tril1, gn), _dot_hi(triu1, gn))
        g_s[n * C:(n + 1) * C] = gc
        gr_s[n] = gc.T

    if use_s0:
        for d in range(2):
            for h in range(DN_HEADS):
                st_s[d * DN_HEADS + h] = s0_ref[0, d, h]
    else:
        st_s[...] = jnp.zeros_like(st_s)

    def step(t, carry):
        loaded = []
        for d in range(2):
            n = t if d == 0 else n_chunks - 1 - t
            r0 = pl.multiple_of(n * C, C)
            r_last = r0 + (C - 1 if d == 0 else 0)
            for h in range(DN_HEADS):
                p = d * DN_HEADS + h
                hs = slice(h * hd, (h + 1) * hd)
                loaded.append((
                    q_s[pl.ds(r0, C), hs], k_s[pl.ds(r0, C), hs], v_s[pl.ds(r0, C), hs],
                    g_s[pl.ds(r0, C), 8 + p:9 + p],
                    gr_s[n, 8 + p:9 + p, :],
                    beta_s[pl.ds(r0, C), p:p + 1],
                    g_s[pl.ds(r_last, 1), 8 + p:9 + p],
                    st_s[p], r0))
        n_p = len(loaded)
        qn, kn, vn, gcol, grow, bcol, glast, s, r0 = (list(col) for col in zip(*loaded))
        incl = [(ci >= cj) if p < DN_HEADS else (ci <= cj) for p in range(n_p)]
        strict = [(ci > cj) if p < DN_HEADS else (ci < cj) for p in range(n_p)]
        decay = [jnp.exp(jnp.where(incl[p], gcol[p] - grow[p], -jnp.inf)) for p in range(n_p)]
        kb = [kn[p] * bcol[p] for p in range(n_p)]
        y = [jnp.where(strict[p], -(_dot_nt(kb[p], kn[p]) * decay[p]), 0.0) for p in range(n_p)]
        attn = [_dot_nt(qn[p], kn[p]) * decay[p] for p in range(n_p)]
        q = [_dot_split(y[p], y[p]) for p in range(n_p)]
        for _ in range(4):
            both = [_dot_split(jnp.concatenate([q[p], y[p]], axis=0), q[p]) for p in range(n_p)]
            y = [y[p] + q[p] + both[p][C:] for p in range(n_p)]
            q = [both[p][:C] for p in range(n_p)]
        yq = [_dot_split(y[p], q[p]) for p in range(n_p)]
        y = [y[p] + q[p] + yq[p] for p in range(n_p)]
        egc = [jnp.exp(gcol[p]) for p in range(n_p)]
        rhs = [jnp.concatenate([vn[p] * bcol[p], kb[p] * egc[p]], axis=1) for p in range(n_p)]
        sol = [rhs[p] + _dot_split(y[p], rhs[p]) for p in range(n_p)]
        v_new = [sol[p][:, :hd] - _dot(sol[p][:, hd:], s[p]) for p in range(n_p)]
        o_old = [_dot(qn[p] * egc[p], s[p]) for p in range(n_p)]
        o = [o_old[p] + _dot(attn[p], v_new[p]) for p in range(n_p)]
        s_new = [s[p] * jnp.exp(glast[p]) + _dot_tn(kn[p] * jnp.exp(glast[p] - gcol[p]), v_new[p])
                 for p in range(n_p)]
        for p in range(n_p):
            d, h = divmod(p, DN_HEADS)
            st_s[p] = s_new[p]
            (of_s if d == 0 else ob_s)[pl.ds(r0[p], C), h * hd:(h + 1) * hd] = o[p]
        return carry

    lax.fori_loop(0, n_chunks, step, 0)

    for h in range(DN_HEADS):
        hs = slice(h * hd, (h + 1) * hd)
        o_ref[:, hs] = _rms(of_s[:, hs] + ob_s[:, hs], norm_ref[...]) * _silu(z_ref[:, hs])
    if write_state:
        for d in range(2):
            for h in range(DN_HEADS):
                sfin_ref[0, d, h] = st_s[d * DN_HEADS + h]


def _deltanet(proj, conv_w, par, norm, s0, prev_out, *, seq, n_seq, row0):
    use_s0 = s0 is not None
    write_state = not use_s0
    blk0 = row0 // seq
    half = MIX_HALF
    in_specs = [
        pl.BlockSpec((seq, half), lambda b: (blk0 + b, 0)),
        pl.BlockSpec((seq, half), lambda b: (blk0 + b, 1)),
        pl.BlockSpec((seq, half), lambda b: (blk0 + b, 2)),
        pl.BlockSpec((seq, half), lambda b: (blk0 + b, 6)),
        pl.BlockSpec((seq, LANES), lambda b: (blk0 + b, 28)),
        pl.BlockSpec((3, half), lambda b: (0, 0)),
        pl.BlockSpec((3, half), lambda b: (0, 1)),
        pl.BlockSpec((3, half), lambda b: (0, 2)),
        pl.BlockSpec((8, LANES), lambda b: (0, 0)),
        pl.BlockSpec((1, HEAD_DIM), lambda b: (0, 0)),
    ]
    args = [proj, proj, proj, proj, proj, conv_w, conv_w, conv_w, par, norm]
    if use_s0:
        in_specs.append(pl.BlockSpec((1, 2, DN_HEADS, HEAD_DIM, HEAD_DIM), lambda b: (b, 0, 0, 0, 0)))
        args.append(s0)
    aliases = {}
    if prev_out is not None:
        in_specs.append(pl.BlockSpec(memory_space=pl.ANY))
        args.append(prev_out)
        aliases = {len(args) - 1: 0}
    out_specs = [pl.BlockSpec((seq, half), lambda b: (blk0 + b, 0))]
    out_shape = [jax.ShapeDtypeStruct((T_ALL, half), F32)]
    if write_state:
        out_specs.append(pl.BlockSpec((1, 2, DN_HEADS, HEAD_DIM, HEAD_DIM), lambda b: (b, 0, 0, 0, 0)))
        out_shape.append(jax.ShapeDtypeStruct((n_seq, 2, DN_HEADS, HEAD_DIM, HEAD_DIM), F32))

    def body(*refs):
        refs = list(refs)
        if prev_out is not None:
            n_in = len(args)
            del refs[n_in - 1]
        _dn_kernel(*refs, seq=seq, use_s0=use_s0, write_state=write_state)

    n_chunks = seq // DN_CHUNK
    res = pl.pallas_call(
        body,
        grid=(n_seq,),
        in_specs=in_specs,
        out_specs=out_specs,
        out_shape=out_shape,
        scratch_shapes=[
            pltpu.VMEM((seq, half), F32),
            pltpu.VMEM((seq, half), F32),
            pltpu.VMEM((seq, half), F32),
            pltpu.VMEM((seq, half), F32),
            pltpu.VMEM((seq, half), F32),
            pltpu.VMEM((seq, LANES), F32),
            pltpu.VMEM((n_chunks, LANES, DN_CHUNK), F32),
            pltpu.VMEM((seq, LANES), F32),
            pltpu.VMEM((2 * DN_HEADS, HEAD_DIM, HEAD_DIM), F32),
        ],
        input_output_aliases=aliases,
        compiler_params=_params(("arbitrary",)),
        name="deltanet_%d" % seq,
    )(*args)
    return res


@functools.lru_cache(maxsize=None)
def _dft_tables(seq):
    k = np.arange(seq, dtype=np.int64)[:, None]
    s = np.arange(seq, dtype=np.int64)[None, :]
    ang = np.pi * ((k * s) % (2 * seq)).astype(np.float64) / seq
    cos = np.cos(ang)
    sin = np.sin(ang)
    fwd = np.concatenate([cos, -sin], axis=0)
    fwd[seq, :] = np.where(np.arange(seq) % 2 == 0, 1.0, -1.0)
    wk = np.full((seq,), 2.0)
    wk[0] = 1.0
    inv = np.concatenate([cos.T * wk[None, :], -2.0 * sin.T], axis=1) / (2.0 * seq)
    inv[:, seq] = np.where(np.arange(seq) % 2 == 0, 1.0, -1.0) / (2.0 * seq)
    return fwd.astype(np.float32), inv.astype(np.float32)


def _hy_filter_kernel(feat_ref, w1_ref, b1_ref, f1_ref, w2_ref, b2_ref, f2_ref, w3f_ref, w3b_ref,
                      dec_ref, fwd_ref, o_ref, *, seq):
    z = jnp.sin(f1_ref[...] * (_dot(feat_ref[...], w1_ref[...]) + b1_ref[...]))
    z = jnp.sin(f2_ref[...] * (_dot(z, w2_ref[...]) + b2_ref[...]))
    dec = dec_ref[...]
    h_fwd = _dot(z, w3f_ref[...]) * dec
    h_bwd = _dot(z, w3b_ref[...]) * dec
    rows = lax.broadcasted_iota(jnp.int32, (seq, 1), 0)
    h_bwd = jnp.where(rows == 0, 0.0, h_bwd)
    spec = _dot_split(fwd_ref[...], jnp.concatenate([h_fwd, h_bwd], axis=1))
    g = h_fwd.shape[1]
    srow = lax.broadcasted_iota(jnp.int32, (2 * seq, 1), 0)
    sign = jnp.where(srow <= seq, 1.0, -1.0)
    o_ref[0] = spec[:, :g] + sign * spec[:, g:]


def _hyena_filters(seq, w1, b1, f1, w2, b2, f2, w3):
    t = jnp.linspace(0.0, 1.0, seq, dtype=F32)[:, None]
    w = (2.0 * math.pi / seq) * jnp.arange(seq, dtype=F32)[:, None]
    f = jnp.linspace(1e-4, HY_BANDS - 1, HY_BANDS, dtype=F32)[None, :]
    feats = jnp.concatenate([t, jnp.cos(f * w), -jnp.sin(f * w)], axis=-1)
    n_feat = feats.shape[1]
    feats = jnp.pad(feats, ((0, 0), (0, LANES - n_feat)))
    w1p = jnp.pad(w1, ((0, LANES - n_feat), (0, 0)))
    deltas = jnp.abs(jnp.linspace(math.log(HY_TARGET) / HY_DECAY_PCT_LONG,
                                  math.log(HY_TARGET) / HY_DECAY_PCT_SHORT, HY_WIDTH, dtype=F32))
    dec = jnp.exp(-t * deltas)
    fwd, _ = _dft_tables(seq)
    g = HY_GROUP
    n_g = HY_WIDTH // g
    small = lambda shape: pl.BlockSpec(shape, lambda o, c: (0,) * len(shape))
    return pl.pallas_call(
        functools.partial(_hy_filter_kernel, seq=seq),
        grid=(2, n_g),
        in_specs=[
            small((seq, LANES)), small((LANES, HY_FFN)), small((1, HY_FFN)), small((1, HY_FFN)),
            small((HY_FFN, HY_FFN)), small((1, HY_FFN)), small((1, HY_FFN)),
            pl.BlockSpec((HY_FFN, g), lambda o, c: (0, (2 * o) * n_g + c)),
            pl.BlockSpec((HY_FFN, g), lambda o, c: (0, (2 * o + 1) * n_g + c)),
            pl.BlockSpec((seq, g), lambda o, c: (0, c)),
            small((2 * seq, seq)),
        ],
        out_specs=pl.BlockSpec((1, 2 * seq, g), lambda o, c: (o, 0, c)),
        out_shape=jax.ShapeDtypeStruct((2, 2 * seq, HY_WIDTH), F32),
        compiler_params=_params(("arbitrary", "arbitrary")),
        name="hyena_filters_%d" % seq,
    )(feats, w1p, b1.reshape(1, -1), f1.reshape(1, -1), w2, b2.reshape(1, -1), f2.reshape(1, -1),
      w3, w3, dec, jnp.asarray(fwd))


def _hy_kernel(x1_ref, x2_ref, z_ref, cw1_ref, cw2_ref, cwz_ref, cb1_ref, cb2_ref, cbz_ref,
               spec_ref, bias_ref, fwd_ref, inv_ref, o_ref, *, seq):
    rows = lax.broadcasted_iota(jnp.int32, (seq, 1), 0)
    first = rows == 0
    last = rows == seq - 1
    gates = (_conv3(x1_ref[...], cw1_ref[...], first, last, seq) + cb1_ref[...],
             _conv3(x2_ref[...], cw2_ref[...], first, last, seq) + cb2_ref[...])
    z = _conv3(z_ref[...], cwz_ref[...], first, last, seq) + cbz_ref[...]
    for o in range(2):
        zf = jnp.dot(fwd_ref[...], z.astype(BF16), preferred_element_type=F32)
        re, im = zf[:seq], zf[seq:]
        hre, him = spec_ref[o, :seq, :], spec_ref[o, seq:, :]
        yre = re * hre - jnp.where(first, 0.0, im * him)
        yim = jnp.where(first, im * him, re * him + im * hre)
        y = jnp.dot(inv_ref[...], jnp.concatenate([yre, yim], axis=0).astype(BF16),
                    preferred_element_type=F32)
        z = gates[o] * (y + z * bias_ref[o:o + 1, :])
    o_ref[...] = z


def _hyena(proj, conv_w, conv_b, spec, bias, prev_out, *, seq, n_seq, row0):
    g = HY_GROUP
    n_g = HY_WIDTH // g
    blk0 = row0 // seq
    col0 = 3 * MIX_HALF // g
    fwd, inv = _dft_tables(seq)
    conv_b = conv_b.reshape(1, -1)

    def pcol(part):
        return pl.BlockSpec((seq, g), lambda c, b: (blk0 + b, col0 + part * n_g + c))

    def wcol(rows_, part):
        return pl.BlockSpec((rows_, g), lambda c, b: (0, part * n_g + c))

    in_specs = [pcol(0), pcol(1), pcol(2), wcol(3, 0), wcol(3, 1), wcol(3, 2),
                wcol(1, 0), wcol(1, 1), wcol(1, 2),
                pl.BlockSpec((2, 2 * seq, g), lambda c, b: (0, 0, c)),
                pl.BlockSpec((2, g), lambda c, b: (0, c)),
                pl.BlockSpec((2 * seq, seq), lambda c, b: (0, 0)),
                pl.BlockSpec((seq, 2 * seq), lambda c, b: (0, 0))]
    args = [proj, proj, proj, conv_w, conv_w, conv_w, conv_b, conv_b, conv_b, spec, bias,
            jnp.asarray(fwd).astype(BF16), jnp.asarray(inv).astype(BF16)]
    aliases = {}
    if prev_out is not None:
        in_specs.append(pl.BlockSpec(memory_space=pl.ANY))
        args.append(prev_out)
        aliases = {len(args) - 1: 0}

    def body(*refs):
        refs = list(refs)
        if prev_out is not None:
            del refs[len(args) - 1]
        _hy_kernel(*refs, seq=seq)

    return pl.pallas_call(
        body,
        grid=(n_g, n_seq),
        in_specs=in_specs,
        out_specs=pl.BlockSpec((seq, g), lambda c, b: (blk0 + b, c)),
        out_shape=jax.ShapeDtypeStruct((T_ALL, HY_WIDTH), F32),
        input_output_aliases=aliases,
        compiler_params=_params(("arbitrary", "arbitrary")),
        name="hyena_%d" % seq,
    )(*args)


def _softmax_av(s, v, sink):
    m = jnp.max(s, axis=-1, keepdims=True)
    if sink is not None:
        m = jnp.maximum(m, sink)
    e = jnp.exp(s - m)
    den = jnp.sum(e, axis=-1, keepdims=True)
    if sink is not None:
        den = den + jnp.exp(sink - m)
    return _dot(e, v) / den


def _attn_ctx_kernel(p_ref, qn_ref, kn_ref, sink_ref, o_ref, kc_ref, vc_ref, kd_ref, vd_ref):
    hd = HEAD_DIM
    scale = hd ** -0.5
    kv_w = KV_HEADS * hd
    q_w = KV_HEADS * Q_GROUP * hd
    vc_ref[0] = p_ref[:, q_w + kv_w:q_w + 2 * kv_w]
    base_d = q_w + 2 * kv_w
    kd_ref[0] = p_ref[:, base_d + q_w:base_d + q_w + kv_w]
    vd_ref[0] = p_ref[:, base_d + q_w + kv_w:base_d + q_w + 2 * kv_w]
    for kv in range(KV_HEADS):
        ks = slice(kv * hd, (kv + 1) * hd)
        k = _rms(p_ref[:, q_w + kv * hd:q_w + (kv + 1) * hd], kn_ref[...])
        kc_ref[0, :, ks] = k
        v = p_ref[:, q_w + kv_w + kv * hd:q_w + kv_w + (kv + 1) * hd]
        for g in range(Q_GROUP):
            h = kv * Q_GROUP + g
            q = _rms(p_ref[:, h * hd:(h + 1) * hd], qn_ref[...]) * scale
            o_ref[:, h * hd:(h + 1) * hd] = _softmax_av(_dot_nt(q, k), v, None)
        k = p_ref[:, base_d + q_w + kv * hd:base_d + q_w + (kv + 1) * hd]
        v = p_ref[:, base_d + q_w + kv_w + kv * hd:base_d + q_w + kv_w + (kv + 1) * hd]
        for g in range(Q_GROUP):
            h = kv * Q_GROUP + g
            q = p_ref[:, base_d + h * hd:base_d + (h + 1) * hd] * scale
            o_ref[:, q_w + h * hd:q_w + (h + 1) * hd] = _softmax_av(
                _dot_nt(q, k), v, sink_ref[0:1, h:h + 1])


def _attn_ctx(proj, q_norm, k_norm, sink):
    kv_w = KV_HEADS * HEAD_DIM
    cache = jax.ShapeDtypeStruct((BATCH, SEQ, kv_w), F32)
    cache_spec = pl.BlockSpec((1, SEQ, kv_w), lambda b: (b, 0, 0))
    return pl.pallas_call(
        _attn_ctx_kernel,
        grid=(BATCH,),
        in_specs=[
            pl.BlockSpec((SEQ, ODD_COLS), lambda b: (b, 0)),
            pl.BlockSpec((1, HEAD_DIM), lambda b: (0, 0)),
            pl.BlockSpec((1, HEAD_DIM), lambda b: (0, 0)),
            pl.BlockSpec((1, LANES), lambda b: (0, 0)),
        ],
        out_specs=[pl.BlockSpec((SEQ, D_MODEL), lambda b: (b, 0)),
                   cache_spec, cache_spec, cache_spec, cache_spec],
        out_shape=[jax.ShapeDtypeStruct((T_ALL, D_MODEL), F32), cache, cache, cache, cache],
        compiler_params=_params(("arbitrary",)),
        name="attn_ctx",
    )(proj, q_norm.reshape(1, -1), k_norm.reshape(1, -1), sink)


def _rope(x, cos, sin_signed):
    lane = lax.broadcasted_iota(jnp.int32, (1, HEAD_DIM), 1)
    lo = (lane % 64) < 32
    partner = jnp.where(lo, pltpu.roll(x, HEAD_DIM - 32, 1), pltpu.roll(x, 32, 1))
    return x * cos + partner * sin_signed


def _attn_lat_kernel(p_ref, ckc_ref, cvc_ref, ckd_ref, cvd_ref, qn_ref, kn_ref, sink_ref,
                     cos_ref, sin_ref, o_ref, k_s, v_s, q_s, *, q_blk):
    hd = HEAD_DIM
    scale = hd ** -0.5
    seq = DEC_SEQ
    kv_w = KV_HEADS * hd
    q_w = KV_HEADS * Q_GROUP * hd
    n_keys = PAST_LEN + seq
    n_blk = seq // q_blk
    cos = cos_ref[...]
    sin = sin_ref[...]
    kpos = lax.broadcasted_iota(jnp.int32, (1, n_keys), 1) - PAST_LEN
    qrow = lax.broadcasted_iota(jnp.int32, (Q_GROUP * q_blk, 1), 0) % q_blk
    for mixer in range(2):
        base = mixer * (q_w + 2 * kv_w)
        ck_ref, cv_ref = (ckc_ref, cvc_ref) if mixer == 0 else (ckd_ref, cvd_ref)
        for kv in range(KV_HEADS):
            ks = slice(kv * hd, (kv + 1) * hd)
            k = p_ref[:, base + q_w + kv * hd:base + q_w + (kv + 1) * hd]
            if mixer == 0:
                k = _rms(k, kn_ref[...])
            k_s[:PAST_LEN, :] = ck_ref[0, :, ks].astype(BF16)
            k_s[PAST_LEN:, :] = _rope(k, cos, sin).astype(BF16)
            v_s[:PAST_LEN, :] = cv_ref[0, :, ks].astype(BF16)
            v_s[PAST_LEN:, :] = p_ref[:, base + q_w + kv_w + kv * hd:
                                      base + q_w + kv_w + (kv + 1) * hd].astype(BF16)
            for g in range(Q_GROUP):
                h = kv * Q_GROUP + g
                q = p_ref[:, base + h * hd:base + (h + 1) * hd]
                if mixer == 0:
                    q = _rms(q, qn_ref[...])
                q = _rope(q, cos, sin) * scale
                for b in range(n_blk):
                    q_s[b, g * q_blk:(g + 1) * q_blk, :] = q[b * q_blk:(b + 1) * q_blk].astype(BF16)

            def blk(b, carry):
                s = lax.dot_general(q_s[b], k_s[...], (((1,), (1,)), ((), ())),
                                    preferred_element_type=F32)
                if mixer == 1:
                    qpos = b * q_blk + qrow
                    ok = (kpos < 0) | (jnp.abs(kpos - qpos) <= WINDOW)
                    s = jnp.where(ok, s, NEG_BIG)
                r0 = pl.multiple_of(b * q_blk, q_blk)
                for g in range(Q_GROUP):
                    h = kv * Q_GROUP + g
                    sink = sink_ref[0:1, h:h + 1] if mixer == 1 else None
                    o = _softmax_av(s[g * q_blk:(g + 1) * q_blk], v_s[...], sink)
                    o_ref[pl.ds(r0, q_blk), mixer * q_w + h * hd:mixer * q_w + (h + 1) * hd] = o
                return carry

            lax.fori_loop(0, n_blk, blk, 0)


def _attn_lat(proj, ck_c, cv_c, ck_d, cv_d, q_norm, k_norm, sink, cos, sin, prev_out):
    kv_w = KV_HEADS * HEAD_DIM
    q_blk = 256
    blk0 = T_CTX // DEC_SEQ
    cache_spec = pl.BlockSpec((1, PAST_LEN, kv_w), lambda b: (b, 0, 0))
    row = lambda n: pl.BlockSpec((1, n), lambda b: (0, 0))
    tab = pl.BlockSpec((DEC_SEQ, HEAD_DIM), lambda b: (0, 0))

    def body(*refs):
        refs = list(refs)
        del refs[10]
        _attn_lat_kernel(*refs, q_blk=q_blk)

    return pl.pallas_call(
        body,
        grid=(DEC_BATCH,),
        in_specs=[pl.BlockSpec((DEC_SEQ, ODD_COLS), lambda b: (blk0 + b, 0)),
                  cache_spec, cache_spec, cache_spec, cache_spec,
                  row(HEAD_DIM), row(HEAD_DIM), row(LANES), tab, tab,
                  pl.BlockSpec(memory_space=pl.ANY)],
        out_specs=pl.BlockSpec((DEC_SEQ, D_MODEL), lambda b: (blk0 + b, 0)),
        out_shape=jax.ShapeDtypeStruct((T_ALL, D_MODEL), F32),
        scratch_shapes=[pltpu.VMEM((PAST_LEN + DEC_SEQ, HEAD_DIM), BF16),
                        pltpu.VMEM((PAST_LEN + DEC_SEQ, HEAD_DIM), BF16),
                        pltpu.VMEM((DEC_SEQ // q_blk, Q_GROUP * q_blk, HEAD_DIM), BF16)],
        input_output_aliases={10: 0},
        compiler_params=_params(("arbitrary",)),
        name="attn_lat",
    )(proj, ck_c, cv_c, ck_d, cv_d, q_norm.reshape(1, -1), k_norm.reshape(1, -1), sink, cos, sin,
      prev_out)


def _rope_tables(seq):
    rows = seq // GRID_W
    row = jnp.repeat(jnp.arange(rows), GRID_W).astype(F32)
    col = (jnp.arange(rows * GRID_W) % GRID_W).astype(F32)
    half = HEAD_DIM // 2
    inv = ROPE_THETA ** (-jnp.arange(0, half, 2, dtype=F32) / half)
    ar = row[:, None] * inv
    ac = col[:, None] * inv
    cos = jnp.concatenate([jnp.cos(ar), jnp.cos(ar), jnp.cos(ac), jnp.cos(ac)], axis=-1)
    sin = jnp.concatenate([-jnp.sin(ar), jnp.sin(ar), -jnp.sin(ac), jnp.sin(ac)], axis=-1)
    return cos, sin


def _ffn_kernel(*refs, n_mix, tile, final):
    refs = list(refs)
    x_ref = refs[0]
    mix_refs = refs[1:1 + n_mix]
    wo_refs = refs[1 + n_mix:1 + 2 * n_mix]
    pos = 1 + 2 * n_mix
    mod_ref, gain_ref, wup_ref, cw_ref, cb_ref, wdn_ref = refs[pos:pos + 6]
    pos += 6
    fin_ref = None
    if final:
        fin_ref = refs[pos]
        pos += 1
    o_ref = refs[pos]
    h_s, acc_s = refs[pos + 1:]

    m = mod_ref[0]
    mixed = None
    for mr, wr in zip(mix_refs, wo_refs):
        part = jnp.dot(mr[...].astype(BF16), wr[...], preferred_element_type=F32)
        mixed = part if mixed is None else mixed + part
    x1 = x_ref[...] + m[2:3] * mixed
    o_ref[...] = x1
    h_s[...] = (_rms(x1, gain_ref[...]) * (1.0 + m[4:5]) + m[3:4]).astype(BF16)
    acc_s[...] = jnp.zeros_like(acc_s)

    seq_mask = jnp.where(pl.program_id(0) < T_CTX // tile, SEQ - 1, DEC_SEQ - 1)
    pos_in_seq = lax.broadcasted_iota(jnp.int32, (tile, 1), 0) & seq_mask
    first = pos_in_seq == 0
    last = pos_in_seq == seq_mask

    def up(c):
        cols_a = slice(c * FF_CHUNK, (c + 1) * FF_CHUNK)
        cols_b = slice(D_FF + c * FF_CHUNK, D_FF + (c + 1) * FF_CHUNK)
        h = h_s[...]
        return (jnp.dot(h, wup_ref[:, cols_a], preferred_element_type=F32),
                jnp.dot(h, wup_ref[:, cols_b], preferred_element_type=F32))

    ab = up(0)
    for c in range(N_FF_CHUNKS):
        nxt = up(c + 1) if c + 1 < N_FF_CHUNKS else None
        cols = slice(c * FF_CHUNK, (c + 1) * FF_CHUNK)
        a = _conv3(ab[0], cw_ref[:, cols], first, last, tile) + cb_ref[:, cols]
        hid = (_silu(a) * ab[1]).astype(BF16)
        acc_s[...] += jnp.dot(hid, wdn_ref[cols, :], preferred_element_type=F32)
        ab = nxt
    x2 = o_ref[...] + m[5:6] * acc_s[...]
    if final:
        x2 = _rms(x2, fin_ref[...])
    o_ref[...] = x2


def _ffn(x, mixes, w_outs, mod, gain, w_up, conv_w, conv_b, w_down, final_gain):
    tile = 1024
    n_mix = len(mixes)
    final = final_gain is not None
    const = lambda shape: pl.BlockSpec(shape, lambda i: (0,) * len(shape),
                                       pipeline_mode=pl.Buffered(1))
    in_specs = [pl.BlockSpec((tile, D_MODEL), lambda i: (i, 0))]
    in_specs += [pl.BlockSpec((tile, mx.shape[1]), lambda i: (i, 0)) for mx in mixes]
    in_specs += [const(w.shape) for w in w_outs]
    in_specs += [
        pl.BlockSpec((1, 6, D_MODEL), lambda i: (_mod_row(i, tile), 0, 0)),
        const((1, D_MODEL)),
        const((D_MODEL, 2 * D_FF)),
        const((3, D_FF)),
        const((1, D_FF)),
        const((D_FF, D_MODEL)),
    ]
    args = [x, *mixes, *w_outs, mod, gain.reshape(1, D_MODEL), w_up, conv_w, conv_b, w_down]
    if final:
        in_specs.append(const((1, D_MODEL)))
        args.append(final_gain.reshape(1, D_MODEL))
    return pl.pallas_call(
        functools.partial(_ffn_kernel, n_mix=n_mix, tile=tile, final=final),
        grid=(T_ALL // tile,),
        in_specs=in_specs,
        out_specs=pl.BlockSpec((tile, D_MODEL), lambda i: (i, 0)),
        out_shape=jax.ShapeDtypeStruct((T_ALL, D_MODEL), F32),
        scratch_shapes=[pltpu.VMEM((tile, D_MODEL), BF16), pltpu.VMEM((tile, D_MODEL), F32)],
        compiler_params=_params(("arbitrary",)),
        name="outproj_ffn",
    )(*args)


def _even_weight(w):
    ha = MIX_HALF
    n_ba = 4 * DN_HEADS
    main = jnp.concatenate([w[:, :3 * ha], w[:, 4 * ha + n_ba:], w[:, 3 * ha:4 * ha]], axis=1)
    ba = jnp.pad(w[:, 4 * ha:4 * ha + n_ba], ((0, 0), (0, LANES - n_ba)))
    return jnp.concatenate([main, ba], axis=1).astype(BF16)


def kernel(x_prompt, x_sample, state_dn, cache_k_c, cache_v_c, cache_k_d, cache_v_d, c, c_ctx,
           final_norm, w_mod, b_mod, norm_mix, norm_ffn, w_out, ffn_w_up, ffn_conv_w, ffn_conv_b,
           ffn_w_down, ev_w_in, dn_conv_w, dn_a_log, dn_dt_bias, dn_norm, hy_conv_w, hy_conv_b,
           hy_w1, hy_b1, hy_freq1, hy_w2, hy_b2, hy_freq2, hy_w3, hy_bias, od_w_in, c_q_norm,
           c_k_norm, d_sink):
    x = jnp.concatenate([x_prompt.reshape(T_CTX, D_MODEL), x_sample.reshape(T_LAT, D_MODEL)], axis=0)
    cvec = jnp.concatenate([c_ctx[None, :], c, jnp.zeros((MOD_ROWS - 1 - DEC_BATCH, D_MODEL), F32)], axis=0)
    mods = _modulation(cvec, w_mod, b_mod).reshape(DEPTH, MOD_ROWS, 6, D_MODEL)
    cos, sin = _rope_tables(DEC_SEQ)
    kv_w = KV_HEADS * HEAD_DIM

    states, k_c, v_c, k_d, v_d = [], [], [], [], []
    for i in range(DEPTH):
        j = i // 2
        mod = mods[i]
        if i % 2 == 0:
            proj = _inproj(x, mod, norm_mix[i], _even_weight(ev_w_in[j]))
            par = jnp.zeros((8, LANES), F32)
            par = par.at[0, 8:8 + 2 * DN_HEADS].set(dn_a_log[j].reshape(-1))
            par = par.at[1, 8:8 + 2 * DN_HEADS].set(dn_dt_bias[j].reshape(-1))
            norm = dn_norm[j].reshape(1, HEAD_DIM)
            o_mix, s_fin = _deltanet(proj, dn_conv_w[j], par, norm, None, None,
                                     seq=SEQ, n_seq=BATCH, row0=0)
            (o_mix,) = _deltanet(proj, dn_conv_w[j], par, norm, state_dn[:, j], o_mix,
                                 seq=DEC_SEQ, n_seq=DEC_BATCH, row0=T_CTX)
            states.append(s_fin)
            filt = (hy_w1[j], hy_b1[j], hy_freq1[j], hy_w2[j], hy_b2[j], hy_freq2[j], hy_w3[j])
            y_mix = _hyena(proj, hy_conv_w[j], hy_conv_b[j], _hyena_filters(SEQ, *filt), hy_bias[j],
                           None, seq=SEQ, n_seq=BATCH, row0=0)
            y_mix = _hyena(proj, hy_conv_w[j], hy_conv_b[j], _hyena_filters(DEC_SEQ, *filt),
                           hy_bias[j], y_mix, seq=DEC_SEQ, n_seq=DEC_BATCH, row0=T_CTX)
            mixes = [o_mix, y_mix]
            wo = w_out[i].astype(BF16)
            w_outs = [wo[:MIX_HALF], wo[MIX_HALF:]]
        else:
            proj = _inproj(x, mod, norm_mix[i], od_w_in[j].astype(BF16))
            sink = jnp.pad(d_sink[j].reshape(1, -1), ((0, 0), (0, LANES - d_sink.shape[1])))
            mix, kc, vc, kd, vd = _attn_ctx(proj, c_q_norm[j], c_k_norm[j], sink)
            k_c.append(kc)
            v_c.append(vc)
            k_d.append(kd)
            v_d.append(vd)
            flat = lambda t: t[:, j].reshape(DEC_BATCH, PAST_LEN, kv_w)
            mix = _attn_lat(proj, flat(cache_k_c), flat(cache_v_c), flat(cache_k_d), flat(cache_v_d),
                            c_q_norm[j], c_k_norm[j], sink, cos, sin, mix)
            mixes = [mix]
            w_outs = [w_out[i].astype(BF16)]
        x = _ffn(x, mixes, w_outs, mod, norm_ffn[i], ffn_w_up[i].astype(BF16), ffn_conv_w[i],
                 ffn_conv_b[i].reshape(1, D_FF), ffn_w_down[i].astype(BF16),
                 final_norm if i == DEPTH - 1 else None)

    y_prompt = x[:T_CTX].reshape(BATCH, SEQ, D_MODEL)
    y_sample = x[T_CTX:].reshape(DEC_BATCH, DEC_SEQ, D_MODEL)
    heads = lambda ts: jnp.stack(ts, axis=1).reshape(BATCH, len(ts), SEQ, KV_HEADS, HEAD_DIM)
    return (y_prompt, y_sample, jnp.stack(states, axis=1), heads(k_c), heads(v_c), heads(k_d),
            heads(v_d))
```

```python
import functools
import math

import jax
import jax.numpy as jnp
import numpy as np
from jax import lax
from jax.experimental import pallas as pl
from jax.experimental.pallas import tpu as pltpu

F32 = jnp.float32
BF16 = jnp.bfloat16

D_MODEL = 1024
BATCH = 16
SEQ = 256
DEPTH = 4
DEC_BATCH = 8
DEC_SEQ = 1024
PAST_LEN = 256
GRID_W = 64
HEAD_DIM = 128
MIX_HALF = D_MODEL // 2
DN_HEADS = MIX_HALF // HEAD_DIM
DN_CHUNK = 64
HY_WIDTH = MIX_HALF
HY_BANDS = 16
HY_FFN = 64
HY_DECAY_PCT_SHORT = 0.3
HY_DECAY_PCT_LONG = 1.5
HY_TARGET = 1e-2
KV_HEADS = 2
Q_GROUP = 2
WINDOW = 128
ROPE_THETA = 10000.0
D_FF = 2816
EPS = 1e-6
NEG_BIG = -1e30

T_CTX = BATCH * SEQ
T_LAT = DEC_BATCH * DEC_SEQ
T_ALL = T_CTX + T_LAT
MOD_ROWS = 16
LANES = 128
VMEM_LIMIT = 56 * 1024 * 1024
FF_CHUNK = 256
N_FF_CHUNKS = D_FF // FF_CHUNK
HY_GROUP = 256
EVEN_COLS = 3 * MIX_HALF + 3 * HY_WIDTH + MIX_HALF + LANES
ODD_COLS = 2048

HIGHEST = lax.Precision.HIGHEST


def _params(sem):
    return pltpu.CompilerParams(dimension_semantics=sem, vmem_limit_bytes=VMEM_LIMIT)


def _sigmoid(x):
    return 1.0 / (1.0 + jnp.exp(-x))


def _silu(x):
    return x * _sigmoid(x)


def _softplus(x):
    return jnp.maximum(x, 0.0) + jnp.log1p(jnp.exp(-jnp.abs(x)))


def _dot(a, b):
    return jnp.dot(a.astype(BF16), b.astype(BF16), preferred_element_type=F32)


def _dot_nt(a, b):
    return lax.dot_general(a.astype(BF16), b.astype(BF16), (((1,), (1,)), ((), ())),
                           preferred_element_type=F32)


def _dot_tn(a, b):
    return lax.dot_general(a.astype(BF16), b.astype(BF16), (((0,), (0,)), ((), ())),
                           preferred_element_type=F32)


def _dot_split(a, b):
    a_hi = a.astype(BF16)
    b_hi = b.astype(BF16)
    a_lo = (a - a_hi.astype(F32)).astype(BF16)
    b_lo = (b - b_hi.astype(F32)).astype(BF16)
    dot = functools.partial(jnp.dot, preferred_element_type=F32)
    return dot(a_hi, b_hi) + dot(a_hi, b_lo) + dot(a_lo, b_hi)


def _dot_hi(a, b):
    return jnp.dot(a, b, preferred_element_type=F32, precision=HIGHEST)


def _rms(x, gain):
    return x * lax.rsqrt(jnp.mean(x * x, axis=-1, keepdims=True) + EPS) * gain


def _mod_row(i, tile):
    n_ctx = T_CTX // tile
    return jnp.where(i < n_ctx, 0, 1 + (i - n_ctx) // (DEC_SEQ // tile))


def _mod_kernel(c_ref, w_ref, b_ref, o_ref):
    o_ref[0] = _dot(_silu(c_ref[...]), w_ref[0]) + b_ref[0]


def _modulation(cvec, w_mod, b_mod):
    n_tile = 1536
    return pl.pallas_call(
        _mod_kernel,
        grid=(DEPTH, 6 * D_MODEL // n_tile),
        in_specs=[
            pl.BlockSpec((MOD_ROWS, D_MODEL), lambda l, n: (0, 0)),
            pl.BlockSpec((1, D_MODEL, n_tile), lambda l, n: (l, 0, n)),
            pl.BlockSpec((1, 1, n_tile), lambda l, n: (l, 0, n)),
        ],
        out_specs=pl.BlockSpec((1, MOD_ROWS, n_tile), lambda l, n: (l, 0, n)),
        out_shape=jax.ShapeDtypeStruct((DEPTH, MOD_ROWS, 6 * D_MODEL), F32),
        compiler_params=_params(("arbitrary", "arbitrary")),
        name="modulation",
    )(cvec, w_mod, b_mod.reshape(DEPTH, 1, 6 * D_MODEL))


def _inproj_kernel(x_ref, mod_ref, gain_ref, w_ref, o_ref, *, n_chunk):
    m = mod_ref[0]
    h = _rms(x_ref[...], gain_ref[...]) * (1.0 + m[1:2]) + m[0:1]
    hb = h.astype(BF16)
    n_cols = o_ref.shape[1]
    for c0 in range(0, n_cols, n_chunk):
        c1 = min(c0 + n_chunk, n_cols)
        o_ref[:, c0:c1] = jnp.dot(hb, w_ref[:, c0:c1], preferred_element_type=F32)


def _inproj(x, mod, gain, w):
    tile = 512
    n_cols = w.shape[1]
    return pl.pallas_call(
        functools.partial(_inproj_kernel, n_chunk=512),
        grid=(T_ALL // tile,),
        in_specs=[
            pl.BlockSpec((tile, D_MODEL), lambda i: (i, 0)),
            pl.BlockSpec((1, 6, D_MODEL), lambda i: (_mod_row(i, tile), 0, 0)),
            pl.BlockSpec((1, D_MODEL), lambda i: (0, 0)),
            pl.BlockSpec((D_MODEL, n_cols), lambda i: (0, 0)),
        ],
        out_specs=pl.BlockSpec((tile, n_cols), lambda i: (i, 0)),
        out_shape=jax.ShapeDtypeStruct((T_ALL, n_cols), F32),
        compiler_params=_params(("arbitrary",)),
        name="inproj",
    )(x, mod, gain.reshape(1, D_MODEL), w)


def _conv3(x, w, first, last, seq_rows):
    xp = jnp.where(first, 0.0, pltpu.roll(x, 1, 0))
    xn = jnp.where(last, 0.0, pltpu.roll(x, seq_rows - 1, 0))
    return xp * w[0:1] + x * w[1:2] + xn * w[2:3]


def _dn_kernel(*refs, seq, use_s0, write_state):
    refs = list(refs)
    q_ref, k_ref, v_ref, z_ref, ba_ref, cwq_ref, cwk_ref, cwv_ref, par_ref, norm_ref = refs[:10]
    pos = 10
    s0_ref = None
    if use_s0:
        s0_ref = refs[pos]
        pos += 1
    o_ref = refs[pos]
    pos += 1
    sfin_ref = None
    if write_state:
        sfin_ref = refs[pos]
        pos += 1
    q_s, k_s, v_s, of_s, ob_s, g_s, gr_s, beta_s, st_s = refs[pos:]

    C = DN_CHUNK
    n_chunks = seq // C
    hd = HEAD_DIM

    rows = lax.broadcasted_iota(jnp.int32, (seq, 1), 0)
    first = rows == 0
    last = rows == seq - 1
    for h in range(DN_HEADS):
        hs = slice(h * hd, (h + 1) * hd)
        q = _silu(_conv3(q_ref[:, hs], cwq_ref[:, hs], first, last, seq))
        k = _silu(_conv3(k_ref[:, hs], cwk_ref[:, hs], first, last, seq))
        v = _silu(_conv3(v_ref[:, hs], cwv_ref[:, hs], first, last, seq))
        q_s[:, hs] = q * lax.rsqrt(jnp.sum(q * q, axis=-1, keepdims=True) + EPS) * (hd ** -0.5)
        k_s[:, hs] = k * lax.rsqrt(jnp.sum(k * k, axis=-1, keepdims=True) + EPS)
        v_s[:, hs] = v

    ba = ba_ref[...]
    beta_s[...] = _sigmoid(ba)
    g = -jnp.exp(par_ref[0:1]) * _softplus(ba + par_ref[1:2])
    lane = lax.broadcasted_iota(jnp.int32, (1, LANES), 1)
    fwd_lane = lane < 8 + DN_HEADS
    ci = lax.broadcasted_iota(jnp.int32, (C, C), 0)
    cj = lax.broadcasted_iota(jnp.int32, (C, C), 1)
    tril1 = (ci >= cj).astype(F32)
    triu1 = (ci <= cj).astype(F32)
    for n in range(n_chunks):
        gn = g[n * C:(n + 1) * C]
        gc = jnp.where(fwd_lane, _dot_hi(tril1, gn), _dot_hi(triu1, gn))
        g_s[n * C:(n + 1) * C] = gc
        gr_s[n] = gc.T

    if use_s0:
        for d in range(2):
            for h in range(DN_HEADS):
                st_s[d * DN_HEADS + h] = s0_ref[0, d, h]
    else:
        st_s[...] = jnp.zeros_like(st_s)

    def step(t, carry):
        loaded = []
        for d in range(2):
            n = t if d == 0 else n_chunks - 1 - t
            r0 = pl.multiple_of(n * C, C)
            r_last = r0 + (C - 1 if d == 0 else 0)
            for h in range(DN_HEADS):
                p = d * DN_HEADS + h
                hs = slice(h * hd, (h + 1) * hd)
                loaded.append((
                    q_s[pl.ds(r0, C), hs], k_s[pl.ds(r0, C), hs], v_s[pl.ds(r0, C), hs],
                    g_s[pl.ds(r0, C), 8 + p:9 + p],
                    gr_s[n, 8 + p:9 + p, :],
                    beta_s[pl.ds(r0, C), p:p + 1],
                    g_s[pl.ds(r_last, 1), 8 + p:9 + p],
                    st_s[p], r0))
        n_p = len(loaded)
        qn, kn, vn, gcol, grow, bcol, glast, s, r0 = (list(col) for col in zip(*loaded))
        incl = [(ci >= cj) if p < DN_HEADS else (ci <= cj) for p in range(n_p)]
        strict = [(ci > cj) if p < DN_HEADS else (ci < cj) for p in range(n_p)]
        decay = [jnp.exp(jnp.where(incl[p], gcol[p] - grow[p], -jnp.inf)) for p in range(n_p)]
        kb = [kn[p] * bcol[p] for p in range(n_p)]
        y = [jnp.where(strict[p], -(_dot_nt(kb[p], kn[p]) * decay[p]), 0.0) for p in range(n_p)]
        attn = [_dot_nt(qn[p], kn[p]) * decay[p] for p in range(n_p)]
        q = [_dot_split(y[p], y[p]) for p in range(n_p)]
        for _ in range(4):
            both = [_dot_split(jnp.concatenate([q[p], y[p]], axis=0), q[p]) for p in range(n_p)]
            y = [y[p] + q[p] + both[p][C:] for p in range(n_p)]
            q = [both[p][:C] for p in range(n_p)]
        yq = [_dot_split(y[p], q[p]) for p in range(n_p)]
        y = [y[p] + q[p] + yq[p] for p in range(n_p)]
        egc = [jnp.exp(gcol[p]) for p in range(n_p)]
        rhs = [jnp.concatenate([vn[p] * bcol[p], kb[p] * egc[p]], axis=1) for p in range(n_p)]
        sol = [rhs[p] + _dot_split(y[p], rhs[p]) for p in range(n_p)]
        v_new = [sol[p][:, :hd] - _dot(sol[p][:, hd:], s[p]) for p in range(n_p)]
        o_old = [_dot(qn[p] * egc[p], s[p]) for p in range(n_p)]
        o = [o_old[p] + _dot(attn[p], v_new[p]) for p in range(n_p)]
        s_new = [s[p] * jnp.exp(glast[p]) + _dot_tn(kn[p] * jnp.exp(glast[p] - gcol[p]), v_new[p])
                 for p in range(n_p)]
        for p in range(n_p):
            d, h = divmod(p, DN_HEADS)
            st_s[p] = s_new[p]
            (of_s if d == 0 else ob_s)[pl.ds(r0[p], C), h * hd:(h + 1) * hd] = o[p]
        return carry

    lax.fori_loop(0, n_chunks, step, 0)

    for h in range(DN_HEADS):
        hs = slice(h * hd, (h + 1) * hd)
        o_ref[:, hs] = _rms(of_s[:, hs] + ob_s[:, hs], norm_ref[...]) * _silu(z_ref[:, hs])
    if write_state:
        for d in range(2):
            for h in range(DN_HEADS):
                sfin_ref[0, d, h] = st_s[d * DN_HEADS + h]


def _deltanet(proj, conv_w, par, norm, s0, prev_out, *, seq, n_seq, row0):
    use_s0 = s0 is not None
    write_state = not use_s0
    blk0 = row0 // seq
    half = MIX_HALF
    in_specs = [
        pl.BlockSpec((seq, half), lambda b: (blk0 + b, 0)),
        pl.BlockSpec((seq, half), lambda b: (blk0 + b, 1)),
        pl.BlockSpec((seq, half), lambda b: (blk0 + b, 2)),
        pl.BlockSpec((seq, half), lambda b: (blk0 + b, 6)),
        pl.BlockSpec((seq, LANES), lambda b: (blk0 + b, 28)),
        pl.BlockSpec((3, half), lambda b: (0, 0)),
        pl.BlockSpec((3, half), lambda b: (0, 1)),
        pl.BlockSpec((3, half), lambda b: (0, 2)),
        pl.BlockSpec((8, LANES), lambda b: (0, 0)),
        pl.BlockSpec((1, HEAD_DIM), lambda b: (0, 0)),
    ]
    args = [proj, proj, proj, proj, proj, conv_w, conv_w, conv_w, par, norm]
    if use_s0:
        in_specs.append(pl.BlockSpec((1, 2, DN_HEADS, HEAD_DIM, HEAD_DIM), lambda b: (b, 0, 0, 0, 0)))
        args.append(s0)
    aliases = {}
    if prev_out is not None:
        in_specs.append(pl.BlockSpec(memory_space=pl.ANY))
        args.append(prev_out)
        aliases = {len(args) - 1: 0}
    out_specs = [pl.BlockSpec((seq, half), lambda b: (blk0 + b, 0))]
    out_shape = [jax.ShapeDtypeStruct((T_ALL, half), F32)]
    if write_state:
        out_specs.append(pl.BlockSpec((1, 2, DN_HEADS, HEAD_DIM, HEAD_DIM), lambda b: (b, 0, 0, 0, 0)))
        out_shape.append(jax.ShapeDtypeStruct((n_seq, 2, DN_HEADS, HEAD_DIM, HEAD_DIM), F32))

    def body(*refs):
        refs = list(refs)
        if prev_out is not None:
            n_in = len(args)
            del refs[n_in - 1]
        _dn_kernel(*refs, seq=seq, use_s0=use_s0, write_state=write_state)

    n_chunks = seq // DN_CHUNK
    res = pl.pallas_call(
        body,
        grid=(n_seq,),
        in_specs=in_specs,
        out_specs=out_specs,
        out_shape=out_shape,
        scratch_shapes=[
            pltpu.VMEM((seq, half), F32),
            pltpu.VMEM((seq, half), F32),
            pltpu.VMEM((seq, half), F32),
            pltpu.VMEM((seq, half), F32),
            pltpu.VMEM((seq, half), F32),
            pltpu.VMEM((seq, LANES), F32),
            pltpu.VMEM((n_chunks, LANES, DN_CHUNK), F32),
            pltpu.VMEM((seq, LANES), F32),
            pltpu.VMEM((2 * DN_HEADS, HEAD_DIM, HEAD_DIM), F32),
        ],
        input_output_aliases=aliases,
        compiler_params=_params(("arbitrary",)),
        name="deltanet_%d" % seq,
    )(*args)
    return res


@functools.lru_cache(maxsize=None)
def _dft_tables(seq):
    k = np.arange(seq, dtype=np.int64)[:, None]
    s = np.arange(seq, dtype=np.int64)[None, :]
    ang = np.pi * ((k * s) % (2 * seq)).astype(np.float64) / seq
    cos = np.cos(ang)
    sin = np.sin(ang)
    fwd = np.concatenate([cos, -sin], axis=0)
    fwd[seq, :] = np.where(np.arange(seq) % 2 == 0, 1.0, -1.0)
    wk = np.full((seq,), 2.0)
    wk[0] = 1.0
    inv = np.concatenate([cos.T * wk[None, :], -2.0 * sin.T], axis=1) / (2.0 * seq)
    inv[:, seq] = np.where(np.arange(seq) % 2 == 0, 1.0, -1.0) / (2.0 * seq)
    return fwd.astype(np.float32), inv.astype(np.float32)


def _hy_filter_kernel(feat_ref, w1_ref, b1_ref, f1_ref, w2_ref, b2_ref, f2_ref, w3f_ref, w3b_ref,
                      dec_ref, fwd_ref, o_ref, *, seq):
    z = jnp.sin(f1_ref[...] * (_dot(feat_ref[...], w1_ref[...]) + b1_ref[...]))
    z = jnp.sin(f2_ref[...] * (_dot(z, w2_ref[...]) + b2_ref[...]))
    dec = dec_ref[...]
    h_fwd = _dot(z, w3f_ref[...]) * dec
    h_bwd = _dot(z, w3b_ref[...]) * dec
    rows = lax.broadcasted_iota(jnp.int32, (seq, 1), 0)
    h_bwd = jnp.where(rows == 0, 0.0, h_bwd)
    spec = _dot_split(fwd_ref[...], jnp.concatenate([h_fwd, h_bwd], axis=1))
    g = h_fwd.shape[1]
    srow = lax.broadcasted_iota(jnp.int32, (2 * seq, 1), 0)
    sign = jnp.where(srow <= seq, 1.0, -1.0)
    o_ref[0] = spec[:, :g] + sign * spec[:, g:]


def _hyena_filters(seq, w1, b1, f1, w2, b2, f2, w3):
    t = jnp.linspace(0.0, 1.0, seq, dtype=F32)[:, None]
    w = (2.0 * math.pi / seq) * jnp.arange(seq, dtype=F32)[:, None]
    f = jnp.linspace(1e-4, HY_BANDS - 1, HY_BANDS, dtype=F32)[None, :]
    feats = jnp.concatenate([t, jnp.cos(f * w), -jnp.sin(f * w)], axis=-1)
    n_feat = feats.shape[1]
    feats = jnp.pad(feats, ((0, 0), (0, LANES - n_feat)))
    w1p = jnp.pad(w1, ((0, LANES - n_feat), (0, 0)))
    deltas = jnp.abs(jnp.linspace(math.log(HY_TARGET) / HY_DECAY_PCT_LONG,
                                  math.log(HY_TARGET) / HY_DECAY_PCT_SHORT, HY_WIDTH, dtype=F32))
    dec = jnp.exp(-t * deltas)
    fwd, _ = _dft_tables(seq)
    g = HY_GROUP
    n_g = HY_WIDTH // g
    small = lambda shape: pl.BlockSpec(shape, lambda o, c: (0,) * len(shape))
    return pl.pallas_call(
        functools.partial(_hy_filter_kernel, seq=seq),
        grid=(2, n_g),
        in_specs=[
            small((seq, LANES)), small((LANES, HY_FFN)), small((1, HY_FFN)), small((1, HY_FFN)),
            small((HY_FFN, HY_FFN)), small((1, HY_FFN)), small((1, HY_FFN)),
            pl.BlockSpec((HY_FFN, g), lambda o, c: (0, (2 * o) * n_g + c)),
            pl.BlockSpec((HY_FFN, g), lambda o, c: (0, (2 * o + 1) * n_g + c)),
            pl.BlockSpec((seq, g), lambda o, c: (0, c)),
            small((2 * seq, seq)),
        ],
        out_specs=pl.BlockSpec((1, 2 * seq, g), lambda o, c: (o, 0, c)),
        out_shape=jax.ShapeDtypeStruct((2, 2 * seq, HY_WIDTH), F32),
        compiler_params=_params(("arbitrary", "arbitrary")),
        name="hyena_filters_%d" % seq,
    )(feats, w1p, b1.reshape(1, -1), f1.reshape(1, -1), w2, b2.reshape(1, -1), f2.reshape(1, -1),
      w3, w3, dec, jnp.asarray(fwd))


def _hy_kernel(x1_ref, x2_ref, z_ref, cw1_ref, cw2_ref, cwz_ref, cb1_ref, cb2_ref, cbz_ref,
               spec_ref, bias_ref, fwd_ref, inv_ref, o_ref, *, seq):
    rows = lax.broadcasted_iota(jnp.int32, (seq, 1), 0)
    first = rows == 0
    last = rows == seq - 1
    gates = (_conv3(x1_ref[...], cw1_ref[...], first, last, seq) + cb1_ref[...],
             _conv3(x2_ref[...], cw2_ref[...], first, last, seq) + cb2_ref[...])
    z = _conv3(z_ref[...], cwz_ref[...], first, last, seq) + cbz_ref[...]
    for o in range(2):
        zf = jnp.dot(fwd_ref[...], z.astype(BF16), preferred_element_type=F32)
        re, im = zf[:seq], zf[seq:]
        hre, him = spec_ref[o, :seq, :], spec_ref[o, seq:, :]
        yre = re * hre - jnp.where(first, 0.0, im * him)
        yim = jnp.where(first, im * him, re * him + im * hre)
        y = jnp.dot(inv_ref[...], jnp.concatenate([yre, yim], axis=0).astype(BF16),
                    preferred_element_type=F32)
        z = gates[o] * (y + z * bias_ref[o:o + 1, :])
    o_ref[...] = z


def _hyena(proj, conv_w, conv_b, spec, bias, prev_out, *, seq, n_seq, row0):
    g = HY_GROUP
    n_g = HY_WIDTH // g
    blk0 = row0 // seq
    col0 = 3 * MIX_HALF // g
    fwd, inv = _dft_tables(seq)
    conv_b = conv_b.reshape(1, -1)

    def pcol(part):
        return pl.BlockSpec((seq, g), lambda c, b: (blk0 + b, col0 + part * n_g + c))

    def wcol(rows_, part):
        return pl.BlockSpec((rows_, g), lambda c, b: (0, part * n_g + c))

    in_specs = [pcol(0), pcol(1), pcol(2), wcol(3, 0), wcol(3, 1), wcol(3, 2),
                wcol(1, 0), wcol(1, 1), wcol(1, 2),
                pl.BlockSpec((2, 2 * seq, g), lambda c, b: (0, 0, c)),
                pl.BlockSpec((2, g), lambda c, b: (0, c)),
                pl.BlockSpec((2 * seq, seq), lambda c, b: (0, 0)),
                pl.BlockSpec((seq, 2 * seq), lambda c, b: (0, 0))]
    args = [proj, proj, proj, conv_w, conv_w, conv_w, conv_b, conv_b, conv_b, spec, bias,
            jnp.asarray(fwd).astype(BF16), jnp.asarray(inv).astype(BF16)]
    aliases = {}
    if prev_out is not None:
        in_specs.append(pl.BlockSpec(memory_space=pl.ANY))
        args.append(prev_out)
        aliases = {len(args) - 1: 0}

    def body(*refs):
        refs = list(refs)
        if prev_out is not None:
            del refs[len(args) - 1]
        _hy_kernel(*refs, seq=seq)

    return pl.pallas_call(
        body,
        grid=(n_g, n_seq),
        in_specs=in_specs,
        out_specs=pl.BlockSpec((seq, g), lambda c, b: (blk0 + b, c)),
        out_shape=jax.ShapeDtypeStruct((T_ALL, HY_WIDTH), F32),
        input_output_aliases=aliases,
        compiler_params=_params(("arbitrary", "arbitrary")),
        name="hyena_%d" % seq,
    )(*args)


def _softmax_av(s, v, sink):
    m = jnp.max(s, axis=-1, keepdims=True)
    if sink is not None:
        m = jnp.maximum(m, sink)
    e = jnp.exp(s - m)
    den = jnp.sum(e, axis=-1, keepdims=True)
    if sink is not None:
        den = den + jnp.exp(sink - m)
    return _dot(e, v) / den


def _attn_ctx_kernel(p_ref, qn_ref, kn_ref, sink_ref, o_ref, kc_ref, vc_ref, kd_ref, vd_ref):
    hd = HEAD_DIM
    scale = hd ** -0.5
    kv_w = KV_HEADS * hd
    q_w = KV_HEADS * Q_GROUP * hd
    vc_ref[0] = p_ref[:, q_w + kv_w:q_w + 2 * kv_w]
    base_d = q_w + 2 * kv_w
    kd_ref[0] = p_ref[:, base_d + q_w:base_d + q_w + kv_w]
    vd_ref[0] = p_ref[:, base_d + q_w + kv_w:base_d + q_w + 2 * kv_w]
    for kv in range(KV_HEADS):
        ks = slice(kv * hd, (kv + 1) * hd)
        k = _rms(p_ref[:, q_w + kv * hd:q_w + (kv + 1) * hd], kn_ref[...])
        kc_ref[0, :, ks] = k
        v = p_ref[:, q_w + kv_w + kv * hd:q_w + kv_w + (kv + 1) * hd]
        for g in range(Q_GROUP):
            h = kv * Q_GROUP + g
            q = _rms(p_ref[:, h * hd:(h + 1) * hd], qn_ref[...]) * scale
            o_ref[:, h * hd:(h + 1) * hd] = _softmax_av(_dot_nt(q, k), v, None)
        k = p_ref[:, base_d + q_w + kv * hd:base_d + q_w + (kv + 1) * hd]
        v = p_ref[:, base_d + q_w + kv_w + kv * hd:base_d + q_w + kv_w + (kv + 1) * hd]
        for g in range(Q_GROUP):
            h = kv * Q_GROUP + g
            q = p_ref[:, base_d + h * hd:base_d + (h + 1) * hd] * scale
            o_ref[:, q_w + h * hd:q_w + (h + 1) * hd] = _softmax_av(
                _dot_nt(q, k), v, sink_ref[0:1, h:h + 1])


def _attn_ctx(proj, q_norm, k_norm, sink):
    kv_w = KV_HEADS * HEAD_DIM
    cache = jax.ShapeDtypeStruct((BATCH, SEQ, kv_w), F32)
    cache_spec = pl.BlockSpec((1, SEQ, kv_w), lambda b: (b, 0, 0))
    return pl.pallas_call(
        _attn_ctx_kernel,
        grid=(BATCH,),
        in_specs=[
            pl.BlockSpec((SEQ, ODD_COLS), lambda b: (b, 0)),
            pl.BlockSpec((1, HEAD_DIM), lambda b: (0, 0)),
            pl.BlockSpec((1, HEAD_DIM), lambda b: (0, 0)),
            pl.BlockSpec((1, LANES), lambda b: (0, 0)),
        ],
        out_specs=[pl.BlockSpec((SEQ, D_MODEL), lambda b: (b, 0)),
                   cache_spec, cache_spec, cache_spec, cache_spec],
        out_shape=[jax.ShapeDtypeStruct((T_ALL, D_MODEL), F32), cache, cache, cache, cache],
        compiler_params=_params(("arbitrary",)),
        name="attn_ctx",
    )(proj, q_norm.reshape(1, -1), k_norm.reshape(1, -1), sink)


def _rope(x, cos, sin_signed):
    lane = lax.broadcasted_iota(jnp.int32, (1, HEAD_DIM), 1)
    lo = (lane % 64) < 32
    partner = jnp.where(lo, pltpu.roll(x, HEAD_DIM - 32, 1), pltpu.roll(x, 32, 1))
    return x * cos + partner * sin_signed


def _attn_lat_kernel(p_ref, ckc_ref, cvc_ref, ckd_ref, cvd_ref, qn_ref, kn_ref, sink_ref,
                     cos_ref, sin_ref, o_ref, k_s, v_s, q_s, *, q_blk):
    hd = HEAD_DIM
    scale = hd ** -0.5
    seq = DEC_SEQ
    kv_w = KV_HEADS * hd
    q_w = KV_HEADS * Q_GROUP * hd
    n_blk = seq // q_blk
    cos = cos_ref[...]
    sin = sin_ref[...]
    qrow = lax.broadcasted_iota(jnp.int32, (Q_GROUP * q_blk, 1), 0) % q_blk
    for mixer in range(2):
        base = mixer * (q_w + 2 * kv_w)
        ck_ref, cv_ref = (ckc_ref, cvc_ref) if mixer == 0 else (ckd_ref, cvd_ref)
        for kv in range(KV_HEADS):
            ks = slice(kv * hd, (kv + 1) * hd)
            k = p_ref[:, base + q_w + kv * hd:base + q_w + (kv + 1) * hd]
            if mixer == 0:
                k = _rms(k, kn_ref[...])
            k_s[:PAST_LEN, :] = ck_ref[0, :, ks].astype(BF16)
            k_s[PAST_LEN:, :] = _rope(k, cos, sin).astype(BF16)
            v_s[:PAST_LEN, :] = cv_ref[0, :, ks].astype(BF16)
            v_s[PAST_LEN:, :] = p_ref[:, base + q_w + kv_w + kv * hd:
                                      base + q_w + kv_w + (kv + 1) * hd].astype(BF16)
            for g in range(Q_GROUP):
                h = kv * Q_GROUP + g
                q = p_ref[:, base + h * hd:base + (h + 1) * hd]
                if mixer == 0:
                    q = _rms(q, qn_ref[...])
                q = _rope(q, cos, sin) * scale
                for b in range(n_blk):
                    q_s[b, g * q_blk:(g + 1) * q_blk, :] = q[b * q_blk:(b + 1) * q_blk].astype(BF16)

            def win_start(b):
                return pl.multiple_of(jnp.clip(b * q_blk - WINDOW, 0, seq - 2 * q_blk), WINDOW)

            def keys_of(b, ref):
                if mixer == 0:
                    return ref[...]
                return jnp.concatenate(
                    [ref[:PAST_LEN], ref[pl.ds(PAST_LEN + win_start(b), 2 * q_blk), :]], axis=0)

            def blk(b, carry):
                s = lax.dot_general(q_s[b], keys_of(b, k_s), (((1,), (1,)), ((), ())),
                                    preferred_element_type=F32)
                if mixer == 1:
                    kidx = lax.broadcasted_iota(jnp.int32, (1, PAST_LEN + 2 * q_blk), 1) - PAST_LEN
                    qpos = b * q_blk + qrow
                    ok = (kidx < 0) | (jnp.abs(kidx + win_start(b) - qpos) <= WINDOW)
                    s = jnp.where(ok, s, NEG_BIG)
                vals = keys_of(b, v_s)
                r0 = pl.multiple_of(b * q_blk, q_blk)
                for g in range(Q_GROUP):
                    h = kv * Q_GROUP + g
                    sink = sink_ref[0:1, h:h + 1] if mixer == 1 else None
                    o = _softmax_av(s[g * q_blk:(g + 1) * q_blk], vals, sink)
                    o_ref[pl.ds(r0, q_blk), mixer * q_w + h * hd:mixer * q_w + (h + 1) * hd] = o
                return carry

            lax.fori_loop(0, n_blk, blk, 0)


def _attn_lat(proj, ck_c, cv_c, ck_d, cv_d, q_norm, k_norm, sink, cos, sin, prev_out):
    kv_w = KV_HEADS * HEAD_DIM
    q_blk = 256
    blk0 = T_CTX // DEC_SEQ
    cache_spec = pl.BlockSpec((1, PAST_LEN, kv_w), lambda b: (b, 0, 0))
    row = lambda n: pl.BlockSpec((1, n), lambda b: (0, 0))
    tab = pl.BlockSpec((DEC_SEQ, HEAD_DIM), lambda b: (0, 0))

    def body(*refs):
        refs = list(refs)
        del refs[10]
        _attn_lat_kernel(*refs, q_blk=q_blk)

    return pl.pallas_call(
        body,
        grid=(DEC_BATCH,),
        in_specs=[pl.BlockSpec((DEC_SEQ, ODD_COLS), lambda b: (blk0 + b, 0)),
                  cache_spec, cache_spec, cache_spec, cache_spec,
                  row(HEAD_DIM), row(HEAD_DIM), row(LANES), tab, tab,
                  pl.BlockSpec(memory_space=pl.ANY)],
        out_specs=pl.BlockSpec((DEC_SEQ, D_MODEL), lambda b: (blk0 + b, 0)),
        out_shape=jax.ShapeDtypeStruct((T_ALL, D_MODEL), F32),
        scratch_shapes=[pltpu.VMEM((PAST_LEN + DEC_SEQ, HEAD_DIM), BF16),
                        pltpu.VMEM((PAST_LEN + DEC_SEQ, HEAD_DIM), BF16),
                        pltpu.VMEM((DEC_SEQ // q_blk, Q_GROUP * q_blk, HEAD_DIM), BF16)],
        input_output_aliases={10: 0},
        compiler_params=_params(("arbitrary",)),
        name="attn_lat",
    )(proj, ck_c, cv_c, ck_d, cv_d, q_norm.reshape(1, -1), k_norm.reshape(1, -1), sink, cos, sin,
      prev_out)


def _rope_tables(seq):
    rows = seq // GRID_W
    row = jnp.repeat(jnp.arange(rows), GRID_W).astype(F32)
    col = (jnp.arange(rows * GRID_W) % GRID_W).astype(F32)
    half = HEAD_DIM // 2
    inv = ROPE_THETA ** (-jnp.arange(0, half, 2, dtype=F32) / half)
    ar = row[:, None] * inv
    ac = col[:, None] * inv
    cos = jnp.concatenate([jnp.cos(ar), jnp.cos(ar), jnp.cos(ac), jnp.cos(ac)], axis=-1)
    sin = jnp.concatenate([-jnp.sin(ar), jnp.sin(ar), -jnp.sin(ac), jnp.sin(ac)], axis=-1)
    return cos, sin


def _ffn_kernel(*refs, n_mix, tile, final):
    refs = list(refs)
    x_ref = refs[0]
    mix_refs = refs[1:1 + n_mix]
    wo_refs = refs[1 + n_mix:1 + 2 * n_mix]
    pos = 1 + 2 * n_mix
    mod_ref, gain_ref, wup_ref, cw_ref, cb_ref, wdn_ref = refs[pos:pos + 6]
    pos += 6
    fin_ref = None
    if final:
        fin_ref = refs[pos]
        pos += 1
    o_ref = refs[pos]
    h_s, acc_s = refs[pos + 1:]

    m = mod_ref[0]
    mixed = None
    for mr, wr in zip(mix_refs, wo_refs):
        part = jnp.dot(mr[...].astype(BF16), wr[...], preferred_element_type=F32)
        mixed = part if mixed is None else mixed + part
    x1 = x_ref[...] + m[2:3] * mixed
    o_ref[...] = x1
    h_s[...] = (_rms(x1, gain_ref[...]) * (1.0 + m[4:5]) + m[3:4]).astype(BF16)
    acc_s[...] = jnp.zeros_like(acc_s)

    seq_mask = jnp.where(pl.program_id(0) < T_CTX // tile, SEQ - 1, DEC_SEQ - 1)
    pos_in_seq = lax.broadcasted_iota(jnp.int32, (tile, 1), 0) & seq_mask
    first = pos_in_seq == 0
    last = pos_in_seq == seq_mask

    def up(c):
        cols_a = slice(c * FF_CHUNK, (c + 1) * FF_CHUNK)
        cols_b = slice(D_FF + c * FF_CHUNK, D_FF + (c + 1) * FF_CHUNK)
        h = h_s[...]
        return (jnp.dot(h, wup_ref[:, cols_a], preferred_element_type=F32),
                jnp.dot(h, wup_ref[:, cols_b], preferred_element_type=F32))

    ab = up(0)
    for c in range(N_FF_CHUNKS):
        nxt = up(c + 1) if c + 1 < N_FF_CHUNKS else None
        cols = slice(c * FF_CHUNK, (c + 1) * FF_CHUNK)
        a = _conv3(ab[0], cw_ref[:, cols], first, last, tile) + cb_ref[:, cols]
        hid = (_silu(a) * ab[1]).astype(BF16)
        acc_s[...] += jnp.dot(hid, wdn_ref[cols, :], preferred_element_type=F32)
        ab = nxt
    x2 = o_ref[...] + m[5:6] * acc_s[...]
    if final:
        x2 = _rms(x2, fin_ref[...])
    o_ref[...] = x2


def _ffn(x, mixes, w_outs, mod, gain, w_up, conv_w, conv_b, w_down, final_gain):
    tile = 1024
    n_mix = len(mixes)
    final = final_gain is not None
    const = lambda shape: pl.BlockSpec(shape, lambda i: (0,) * len(shape),
                                       pipeline_mode=pl.Buffered(1))
    in_specs = [pl.BlockSpec((tile, D_MODEL), lambda i: (i, 0))]
    in_specs += [pl.BlockSpec((tile, mx.shape[1]), lambda i: (i, 0)) for mx in mixes]
    in_specs += [const(w.shape) for w in w_outs]
    in_specs += [
        pl.BlockSpec((1, 6, D_MODEL), lambda i: (_mod_row(i, tile), 0, 0)),
        const((1, D_MODEL)),
        const((D_MODEL, 2 * D_FF)),
        const((3, D_FF)),
        const((1, D_FF)),
        const((D_FF, D_MODEL)),
    ]
    args = [x, *mixes, *w_outs, mod, gain.reshape(1, D_MODEL), w_up, conv_w, conv_b, w_down]
    if final:
        in_specs.append(const((1, D_MODEL)))
        args.append(final_gain.reshape(1, D_MODEL))
    return pl.pallas_call(
        functools.partial(_ffn_kernel, n_mix=n_mix, tile=tile, final=final),
        grid=(T_ALL // tile,),
        in_specs=in_specs,
        out_specs=pl.BlockSpec((tile, D_MODEL), lambda i: (i, 0)),
        out_shape=jax.ShapeDtypeStruct((T_ALL, D_MODEL), F32),
        scratch_shapes=[pltpu.VMEM((tile, D_MODEL), BF16), pltpu.VMEM((tile, D_MODEL), F32)],
        compiler_params=_params(("arbitrary",)),
        name="outproj_ffn",
    )(*args)


def _even_weight(w):
    ha = MIX_HALF
    n_ba = 4 * DN_HEADS
    main = jnp.concatenate([w[:, :3 * ha], w[:, 4 * ha + n_ba:], w[:, 3 * ha:4 * ha]], axis=1)
    ba = jnp.pad(w[:, 4 * ha:4 * ha + n_ba], ((0, 0), (0, LANES - n_ba)))
    return jnp.concatenate([main, ba], axis=1).astype(BF16)


def kernel(x_prompt, x_sample, state_dn, cache_k_c, cache_v_c, cache_k_d, cache_v_d, c, c_ctx,
           final_norm, w_mod, b_mod, norm_mix, norm_ffn, w_out, ffn_w_up, ffn_conv_w, ffn_conv_b,
           ffn_w_down, ev_w_in, dn_conv_w, dn_a_log, dn_dt_bias, dn_norm, hy_conv_w, hy_conv_b,
           hy_w1, hy_b1, hy_freq1, hy_w2, hy_b2, hy_freq2, hy_w3, hy_bias, od_w_in, c_q_norm,
           c_k_norm, d_sink):
    x = jnp.concatenate([x_prompt.reshape(T_CTX, D_MODEL), x_sample.reshape(T_LAT, D_MODEL)], axis=0)
    cvec = jnp.concatenate([c_ctx[None, :], c, jnp.zeros((MOD_ROWS - 1 - DEC_BATCH, D_MODEL), F32)], axis=0)
    mods = _modulation(cvec, w_mod, b_mod).reshape(DEPTH, MOD_ROWS, 6, D_MODEL)
    cos, sin = _rope_tables(DEC_SEQ)
    kv_w = KV_HEADS * HEAD_DIM

    states, k_c, v_c, k_d, v_d = [], [], [], [], []
    for i in range(DEPTH):
        j = i // 2
        mod = mods[i]
        if i % 2 == 0:
            proj = _inproj(x, mod, norm_mix[i], _even_weight(ev_w_in[j]))
            par = jnp.zeros((8, LANES), F32)
            par = par.at[0, 8:8 + 2 * DN_HEADS].set(dn_a_log[j].reshape(-1))
            par = par.at[1, 8:8 + 2 * DN_HEADS].set(dn_dt_bias[j].reshape(-1))
            norm = dn_norm[j].reshape(1, HEAD_DIM)
            o_mix, s_fin = _deltanet(proj, dn_conv_w[j], par, norm, None, None,
                                     seq=SEQ, n_seq=BATCH, row0=0)
            (o_mix,) = _deltanet(proj, dn_conv_w[j], par, norm, state_dn[:, j], o_mix,
                                 seq=DEC_SEQ, n_seq=DEC_BATCH, row0=T_CTX)
            states.append(s_fin)
            filt = (hy_w1[j], hy_b1[j], hy_freq1[j], hy_w2[j], hy_b2[j], hy_freq2[j], hy_w3[j])
            y_mix = _hyena(proj, hy_conv_w[j], hy_conv_b[j], _hyena_filters(SEQ, *filt), hy_bias[j],
                           None, seq=SEQ, n_seq=BATCH, row0=0)
            y_mix = _hyena(proj, hy_conv_w[j], hy_conv_b[j], _hyena_filters(DEC_SEQ, *filt),
                           hy_bias[j], y_mix, seq=DEC_SEQ, n_seq=DEC_BATCH, row0=T_CTX)
            mixes = [o_mix, y_mix]
            wo = w_out[i].astype(BF16)
            w_outs = [wo[:MIX_HALF], wo[MIX_HALF:]]
        else:
            proj = _inproj(x, mod, norm_mix[i], od_w_in[j].astype(BF16))
            sink = jnp.pad(d_sink[j].reshape(1, -1), ((0, 0), (0, LANES - d_sink.shape[1])))
            mix, kc, vc, kd, vd = _attn_ctx(proj, c_q_norm[j], c_k_norm[j], sink)
            k_c.append(kc)
            v_c.append(vc)
            k_d.append(kd)
            v_d.append(vd)
            flat = lambda t: t[:, j].reshape(DEC_BATCH, PAST_LEN, kv_w)
            mix = _attn_lat(proj, flat(cache_k_c), flat(cache_v_c), flat(cache_k_d), flat(cache_v_d),
                            c_q_norm[j], c_k_norm[j], sink, cos, sin, mix)
            mixes = [mix]
            w_outs = [w_out[i].astype(BF16)]
        x = _ffn(x, mixes, w_outs, mod, norm_ffn[i], ffn_w_up[i].astype(BF16), ffn_conv_w[i],
                 ffn_conv_b[i].reshape(1, D_FF), ffn_w_down[i].astype(BF16),
                 final_norm if i == DEPTH - 1 else None)

    y_prompt = x[:T_CTX].reshape(BATCH, SEQ, D_MODEL)
    y_sample = x[T_CTX:].reshape(DEC_BATCH, DEC_SEQ, D_MODEL)
    heads = lambda ts: jnp.stack(ts, axis=1).reshape(BATCH, len(ts), SEQ, KV_HEADS, HEAD_DIM)
    return (y_prompt, y_sample, jnp.stack(states, axis=1), heads(k_c), heads(v_c), heads(k_d),
            heads(v_d))
```

```python
import functools
import math

import jax
import jax.numpy as jnp
import numpy as np
from jax import lax
from jax.experimental import pallas as pl
from jax.experimental.pallas import tpu as pltpu

F32 = jnp.float32
BF16 = jnp.bfloat16

D_MODEL = 1024
BATCH = 16
SEQ = 256
DEPTH = 4
DEC_BATCH = 8
DEC_SEQ = 1024
PAST_LEN = 256
GRID_W = 64
HEAD_DIM = 128
MIX_HALF = D_MODEL // 2
DN_HEADS = MIX_HALF // HEAD_DIM
DN_CHUNK = 64
HY_WIDTH = MIX_HALF
HY_BANDS = 16
HY_FFN = 64
HY_DECAY_PCT_SHORT = 0.3
HY_DECAY_PCT_LONG = 1.5
HY_TARGET = 1e-2
KV_HEADS = 2
Q_GROUP = 2
WINDOW = 128
ROPE_THETA = 10000.0
D_FF = 2816
EPS = 1e-6
NEG_BIG = -1e30

T_CTX = BATCH * SEQ
T_LAT = DEC_BATCH * DEC_SEQ
T_ALL = T_CTX + T_LAT
MOD_ROWS = 16
LANES = 128
VMEM_LIMIT = 56 * 1024 * 1024
FF_CHUNK = 256
N_FF_CHUNKS = D_FF // FF_CHUNK
HY_GROUP = 256
EVEN_COLS = 3 * MIX_HALF + 3 * HY_WIDTH + MIX_HALF + LANES
ODD_COLS = 2048

HIGHEST = lax.Precision.HIGHEST


def _params(sem):
    return pltpu.CompilerParams(dimension_semantics=sem, vmem_limit_bytes=VMEM_LIMIT)


def _sigmoid(x):
    return 1.0 / (1.0 + jnp.exp(-x))


def _silu(x):
    return x * _sigmoid(x)


def _softplus(x):
    return jnp.maximum(x, 0.0) + jnp.log1p(jnp.exp(-jnp.abs(x)))


def _dot(a, b):
    return jnp.dot(a.astype(BF16), b.astype(BF16), preferred_element_type=F32)


def _dot_nt(a, b):
    return lax.dot_general(a.astype(BF16), b.astype(BF16), (((1,), (1,)), ((), ())),
                           preferred_element_type=F32)


def _dot_tn(a, b):
    return lax.dot_general(a.astype(BF16), b.astype(BF16), (((0,), (0,)), ((), ())),
                           preferred_element_type=F32)


def _dot_split(a, b):
    a_hi = a.astype(BF16)
    b_hi = b.astype(BF16)
    a_lo = (a - a_hi.astype(F32)).astype(BF16)
    b_lo = (b - b_hi.astype(F32)).astype(BF16)
    dot = functools.partial(jnp.dot, preferred_element_type=F32)
    return dot(a_hi, b_hi) + dot(a_hi, b_lo) + dot(a_lo, b_hi)


def _dot_hi(a, b):
    return jnp.dot(a, b, preferred_element_type=F32, precision=HIGHEST)


def _rms(x, gain):
    return x * lax.rsqrt(jnp.mean(x * x, axis=-1, keepdims=True) + EPS) * gain


def _mod_row(i, tile):
    n_ctx = T_CTX // tile
    return jnp.where(i < n_ctx, 0, 1 + (i - n_ctx) // (DEC_SEQ // tile))


def _mod_kernel(c_ref, w_ref, b_ref, o_ref):
    o_ref[0] = _dot(_silu(c_ref[...]), w_ref[0]) + b_ref[0]


def _modulation(cvec, w_mod, b_mod):
    n_tile = 1536
    return pl.pallas_call(
        _mod_kernel,
        grid=(DEPTH, 6 * D_MODEL // n_tile),
        in_specs=[
            pl.BlockSpec((MOD_ROWS, D_MODEL), lambda l, n: (0, 0)),
            pl.BlockSpec((1, D_MODEL, n_tile), lambda l, n: (l, 0, n)),
            pl.BlockSpec((1, 1, n_tile), lambda l, n: (l, 0, n)),
        ],
        out_specs=pl.BlockSpec((1, MOD_ROWS, n_tile), lambda l, n: (l, 0, n)),
        out_shape=jax.ShapeDtypeStruct((DEPTH, MOD_ROWS, 6 * D_MODEL), F32),
        compiler_params=_params(("arbitrary", "arbitrary")),
        name="modulation",
    )(cvec, w_mod, b_mod.reshape(DEPTH, 1, 6 * D_MODEL))


def _inproj_kernel(x_ref, mod_ref, gain_ref, w_ref, o_ref, *, n_chunk):
    m = mod_ref[0]
    h = _rms(x_ref[...], gain_ref[...]) * (1.0 + m[1:2]) + m[0:1]
    hb = h.astype(BF16)
    n_cols = o_ref.shape[1]
    for c0 in range(0, n_cols, n_chunk):
        c1 = min(c0 + n_chunk, n_cols)
        o_ref[:, c0:c1] = jnp.dot(hb, w_ref[:, c0:c1], preferred_element_type=F32)


def _inproj(x, mod, gain, w):
    tile = 512
    n_cols = w.shape[1]
    return pl.pallas_call(
        functools.partial(_inproj_kernel, n_chunk=512),
        grid=(T_ALL // tile,),
        in_specs=[
            pl.BlockSpec((tile, D_MODEL), lambda i: (i, 0)),
            pl.BlockSpec((1, 6, D_MODEL), lambda i: (_mod_row(i, tile), 0, 0)),
            pl.BlockSpec((1, D_MODEL), lambda i: (0, 0)),
            pl.BlockSpec((D_MODEL, n_cols), lambda i: (0, 0)),
        ],
        out_specs=pl.BlockSpec((tile, n_cols), lambda i: (i, 0)),
        out_shape=jax.ShapeDtypeStruct((T_ALL, n_cols), F32),
        compiler_params=_params(("arbitrary",)),
        name="inproj",
    )(x, mod, gain.reshape(1, D_MODEL), w)


def _conv3(x, w, first, last, seq_rows):
    xp = jnp.where(first, 0.0, pltpu.roll(x, 1, 0))
    xn = jnp.where(last, 0.0, pltpu.roll(x, seq_rows - 1, 0))
    return xp * w[0:1] + x * w[1:2] + xn * w[2:3]


def _dn_kernel(*refs, seq, use_s0, write_state):
    refs = list(refs)
    q_ref, k_ref, v_ref, z_ref, ba_ref, cwq_ref, cwk_ref, cwv_ref, par_ref, norm_ref = refs[:10]
    pos = 10
    s0_ref = None
    if use_s0:
        s0_ref = refs[pos]
        pos += 1
    o_ref = refs[pos]
    pos += 1
    sfin_ref = None
    if write_state:
        sfin_ref = refs[pos]
        pos += 1
    q_s, k_s, v_s, of_s, ob_s, g_s, gr_s, beta_s, st_s = refs[pos:]

    C = DN_CHUNK
    n_chunks = seq // C
    hd = HEAD_DIM

    rows = lax.broadcasted_iota(jnp.int32, (seq, 1), 0)
    first = rows == 0
    last = rows == seq - 1
    for h in range(DN_HEADS):
        hs = slice(h * hd, (h + 1) * hd)
        q = _silu(_conv3(q_ref[:, hs], cwq_ref[:, hs], first, last, seq))
        k = _silu(_conv3(k_ref[:, hs], cwk_ref[:, hs], first, last, seq))
        v = _silu(_conv3(v_ref[:, hs], cwv_ref[:, hs], first, last, seq))
        q_s[:, hs] = q * lax.rsqrt(jnp.sum(q * q, axis=-1, keepdims=True) + EPS) * (hd ** -0.5)
        k_s[:, hs] = k * lax.rsqrt(jnp.sum(k * k, axis=-1, keepdims=True) + EPS)
        v_s[:, hs] = v

    ba = ba_ref[...]
    beta_s[...] = _sigmoid(ba)
    g = -jnp.exp(par_ref[0:1]) * _softplus(ba + par_ref[1:2])
    lane = lax.broadcasted_iota(jnp.int32, (1, LANES), 1)
    fwd_lane = lane < 8 + DN_HEADS
    ci = lax.broadcasted_iota(jnp.int32, (C, C), 0)
    cj = lax.broadcasted_iota(jnp.int32, (C, C), 1)
    tril1 = (ci >= cj).astype(F32)
    triu1 = (ci <= cj).astype(F32)
    for n in range(n_chunks):
        gn = g[n * C:(n + 1) * C]
        gc = jnp.where(fwd_lane, _dot_hi(tril1, gn), _dot_hi(triu1, gn))
        g_s[n * C:(n + 1) * C] = gc
        gr_s[n] = gc.T

    if use_s0:
        for d in range(2):
            for h in range(DN_HEADS):
                st_s[d * DN_HEADS + h] = s0_ref[0, d, h]
    else:
        st_s[...] = jnp.zeros_like(st_s)

    def step(t, carry):
        loaded = []
        for d in range(2):
            n = t if d == 0 else n_chunks - 1 - t
            r0 = pl.multiple_of(n * C, C)
            r_last = r0 + (C - 1 if d == 0 else 0)
            for h in range(DN_HEADS):
                p = d * DN_HEADS + h
                hs = slice(h * hd, (h + 1) * hd)
                loaded.append((
                    q_s[pl.ds(r0, C), hs], k_s[pl.ds(r0, C), hs], v_s[pl.ds(r0, C), hs],
                    g_s[pl.ds(r0, C), 8 + p:9 + p],
                    gr_s[n, 8 + p:9 + p, :],
                    beta_s[pl.ds(r0, C), p:p + 1],
                    g_s[pl.ds(r_last, 1), 8 + p:9 + p],
                    st_s[p], r0))
        n_p = len(loaded)
        qn, kn, vn, gcol, grow, bcol, glast, s, r0 = (list(col) for col in zip(*loaded))
        incl = [(ci >= cj) if p < DN_HEADS else (ci <= cj) for p in range(n_p)]
        strict = [(ci > cj) if p < DN_HEADS else (ci < cj) for p in range(n_p)]
        decay = [jnp.exp(jnp.where(incl[p], gcol[p] - grow[p], -jnp.inf)) for p in range(n_p)]
        kb = [kn[p] * bcol[p] for p in range(n_p)]
        y = [jnp.where(strict[p], -(_dot_nt(kb[p], kn[p]) * decay[p]), 0.0) for p in range(n_p)]
        attn = [_dot_nt(qn[p], kn[p]) * decay[p] for p in range(n_p)]
        q = [_dot_split(y[p], y[p]) for p in range(n_p)]
        for _ in range(4):
            both = [_dot_split(jnp.concatenate([q[p], y[p]], axis=0), q[p]) for p in range(n_p)]
            y = [y[p] + q[p] + both[p][C:] for p in range(n_p)]
            q = [both[p][:C] for p in range(n_p)]
        yq = [_dot_split(y[p], q[p]) for p in range(n_p)]
        y = [y[p] + q[p] + yq[p] for p in range(n_p)]
        egc = [jnp.exp(gcol[p]) for p in range(n_p)]
        rhs = [jnp.concatenate([vn[p] * bcol[p], kb[p] * egc[p]], axis=1) for p in range(n_p)]
        sol = [rhs[p] + _dot_split(y[p], rhs[p]) for p in range(n_p)]
        v_new = [sol[p][:, :hd] - _dot(sol[p][:, hd:], s[p]) for p in range(n_p)]
        o_old = [_dot(qn[p] * egc[p], s[p]) for p in range(n_p)]
        o = [o_old[p] + _dot(attn[p], v_new[p]) for p in range(n_p)]
        s_new = [s[p] * jnp.exp(glast[p]) + _dot_tn(kn[p] * jnp.exp(glast[p] - gcol[p]), v_new[p])
                 for p in range(n_p)]
        for p in range(n_p):
            d, h = divmod(p, DN_HEADS)
            st_s[p] = s_new[p]
            (of_s if d == 0 else ob_s)[pl.ds(r0[p], C), h * hd:(h + 1) * hd] = o[p]
        return carry

    lax.fori_loop(0, n_chunks, step, 0)

    for h in range(DN_HEADS):
        hs = slice(h * hd, (h + 1) * hd)
        o_ref[:, hs] = _rms(of_s[:, hs] + ob_s[:, hs], norm_ref[...]) * _silu(z_ref[:, hs])
    if write_state:
        for d in range(2):
            for h in range(DN_HEADS):
                sfin_ref[0, d, h] = st_s[d * DN_HEADS + h]


def _deltanet(proj, conv_w, par, norm, s0, prev_out, *, seq, n_seq, row0):
    use_s0 = s0 is not None
    write_state = not use_s0
    blk0 = row0 // seq
    half = MIX_HALF
    in_specs = [
        pl.BlockSpec((seq, half), lambda b: (blk0 + b, 0)),
        pl.BlockSpec((seq, half), lambda b: (blk0 + b, 1)),
        pl.BlockSpec((seq, half), lambda b: (blk0 + b, 2)),
        pl.BlockSpec((seq, half), lambda b: (blk0 + b, 6)),
        pl.BlockSpec((seq, LANES), lambda b: (blk0 + b, 28)),
        pl.BlockSpec((3, half), lambda b: (0, 0)),
        pl.BlockSpec((3, half), lambda b: (0, 1)),
        pl.BlockSpec((3, half), lambda b: (0, 2)),
        pl.BlockSpec((8, LANES), lambda b: (0, 0)),
        pl.BlockSpec((1, HEAD_DIM), lambda b: (0, 0)),
    ]
    args = [proj, proj, proj, proj, proj, conv_w, conv_w, conv_w, par, norm]
    if use_s0:
        in_specs.append(pl.BlockSpec((1, 2, DN_HEADS, HEAD_DIM, HEAD_DIM), lambda b: (b, 0, 0, 0, 0)))
        args.append(s0)
    aliases = {}
    if prev_out is not None:
        in_specs.append(pl.BlockSpec(memory_space=pl.ANY))
        args.append(prev_out)
        aliases = {len(args) - 1: 0}
    out_specs = [pl.BlockSpec((seq, half), lambda b: (blk0 + b, 0))]
    out_shape = [jax.ShapeDtypeStruct((T_ALL, half), F32)]
    if write_state:
        out_specs.append(pl.BlockSpec((1, 2, DN_HEADS, HEAD_DIM, HEAD_DIM), lambda b: (b, 0, 0, 0, 0)))
        out_shape.append(jax.ShapeDtypeStruct((n_seq, 2, DN_HEADS, HEAD_DIM, HEAD_DIM), F32))

    def body(*refs):
        refs = list(refs)
        if prev_out is not None:
            n_in = len(args)
            del refs[n_in - 1]
        _dn_kernel(*refs, seq=seq, use_s0=use_s0, write_state=write_state)

    n_chunks = seq // DN_CHUNK
    res = pl.pallas_call(
        body,
        grid=(n_seq,),
        in_specs=in_specs,
        out_specs=out_specs,
        out_shape=out_shape,
        scratch_shapes=[
            pltpu.VMEM((seq, half), F32),
            pltpu.VMEM((seq, half), F32),
            pltpu.VMEM((seq, half), F32),
            pltpu.VMEM((seq, half), F32),
            pltpu.VMEM((seq, half), F32),
            pltpu.VMEM((seq, LANES), F32),
            pltpu.VMEM((n_chunks, LANES, DN_CHUNK), F32),
            pltpu.VMEM((seq, LANES), F32),
            pltpu.VMEM((2 * DN_HEADS, HEAD_DIM, HEAD_DIM), F32),
        ],
        input_output_aliases=aliases,
        compiler_params=_params(("arbitrary",)),
        name="deltanet_%d" % seq,
    )(*args)
    return res


@functools.lru_cache(maxsize=None)
def _dft_tables(seq):
    k = np.arange(seq, dtype=np.int64)[:, None]
    s = np.arange(seq, dtype=np.int64)[None, :]
    ang = np.pi * ((k * s) % (2 * seq)).astype(np.float64) / seq
    cos = np.cos(ang)
    sin = np.sin(ang)
    fwd = np.concatenate([cos, -sin], axis=0)
    fwd[seq, :] = np.where(np.arange(seq) % 2 == 0, 1.0, -1.0)
    wk = np.full((seq,), 2.0)
    wk[0] = 1.0
    inv = np.concatenate([cos.T * wk[None, :], -2.0 * sin.T], axis=1) / (2.0 * seq)
    inv[:, seq] = np.where(np.arange(seq) % 2 == 0, 1.0, -1.0) / (2.0 * seq)
    return fwd.astype(np.float32), inv.astype(np.float32)


def _hy_filter_kernel(feat_ref, w1_ref, b1_ref, f1_ref, w2_ref, b2_ref, f2_ref, w3f_ref, w3b_ref,
                      dec_ref, fwd_ref, o_ref, *, seq):
    z = jnp.sin(f1_ref[...] * (_dot(feat_ref[...], w1_ref[...]) + b1_ref[...]))
    z = jnp.sin(f2_ref[...] * (_dot(z, w2_ref[...]) + b2_ref[...]))
    dec = dec_ref[...]
    h_fwd = _dot(z, w3f_ref[...]) * dec
    h_bwd = _dot(z, w3b_ref[...]) * dec
    rows = lax.broadcasted_iota(jnp.int32, (seq, 1), 0)
    h_bwd = jnp.where(rows == 0, 0.0, h_bwd)
    spec = _dot_split(fwd_ref[...], jnp.concatenate([h_fwd, h_bwd], axis=1))
    g = h_fwd.shape[1]
    srow = lax.broadcasted_iota(jnp.int32, (2 * seq, 1), 0)
    sign = jnp.where(srow <= seq, 1.0, -1.0)
    o_ref[0] = spec[:, :g] + sign * spec[:, g:]


def _hyena_filters(seq, w1, b1, f1, w2, b2, f2, w3):
    t = jnp.linspace(0.0, 1.0, seq, dtype=F32)[:, None]
    w = (2.0 * math.pi / seq) * jnp.arange(seq, dtype=F32)[:, None]
    f = jnp.linspace(1e-4, HY_BANDS - 1, HY_BANDS, dtype=F32)[None, :]
    feats = jnp.concatenate([t, jnp.cos(f * w), -jnp.sin(f * w)], axis=-1)
    n_feat = feats.shape[1]
    feats = jnp.pad(feats, ((0, 0), (0, LANES - n_feat)))
    w1p = jnp.pad(w1, ((0, LANES - n_feat), (0, 0)))
    deltas = jnp.abs(jnp.linspace(math.log(HY_TARGET) / HY_DECAY_PCT_LONG,
                                  math.log(HY_TARGET) / HY_DECAY_PCT_SHORT, HY_WIDTH, dtype=F32))
    dec = jnp.exp(-t * deltas)
    fwd, _ = _dft_tables(seq)
    g = HY_GROUP
    n_g = HY_WIDTH // g
    small = lambda shape: pl.BlockSpec(shape, lambda o, c: (0,) * len(shape))
    return pl.pallas_call(
        functools.partial(_hy_filter_kernel, seq=seq),
        grid=(2, n_g),
        in_specs=[
            small((seq, LANES)), small((LANES, HY_FFN)), small((1, HY_FFN)), small((1, HY_FFN)),
            small((HY_FFN, HY_FFN)), small((1, HY_FFN)), small((1, HY_FFN)),
            pl.BlockSpec((HY_FFN, g), lambda o, c: (0, (2 * o) * n_g + c)),
            pl.BlockSpec((HY_FFN, g), lambda o, c: (0, (2 * o + 1) * n_g + c)),
            pl.BlockSpec((seq, g), lambda o, c: (0, c)),
            small((2 * seq, seq)),
        ],
        out_specs=pl.BlockSpec((1, 2 * seq, g), lambda o, c: (o, 0, c)),
        out_shape=jax.ShapeDtypeStruct((2, 2 * seq, HY_WIDTH), F32),
        compiler_params=_params(("arbitrary", "arbitrary")),
        name="hyena_filters_%d" % seq,
    )(feats, w1p, b1.reshape(1, -1), f1.reshape(1, -1), w2, b2.reshape(1, -1), f2.reshape(1, -1),
      w3, w3, dec, jnp.asarray(fwd))


def _hy_kernel(x1_ref, x2_ref, z_ref, cw1_ref, cw2_ref, cwz_ref, cb1_ref, cb2_ref, cbz_ref,
               spec_ref, bias_ref, fwd_ref, inv_ref, o_ref, *, seq, n_par):
    rows = lax.broadcasted_iota(jnp.int32, (seq, 1), 0)
    first = rows == 0
    last = rows == seq - 1
    par = range(n_par)
    rs = [slice(i * seq, (i + 1) * seq) for i in par]
    gates = [(_conv3(x1_ref[rs[i], :], cw1_ref[...], first, last, seq) + cb1_ref[...],
              _conv3(x2_ref[rs[i], :], cw2_ref[...], first, last, seq) + cb2_ref[...]) for i in par]
    z = [_conv3(z_ref[rs[i], :], cwz_ref[...], first, last, seq) + cbz_ref[...] for i in par]
    for o in range(2):
        zf = [jnp.dot(fwd_ref[...], z[i].astype(BF16), preferred_element_type=F32) for i in par]
        hre, him = spec_ref[o, :seq, :], spec_ref[o, seq:, :]
        prod = []
        for i in par:
            re, im = zf[i][:seq], zf[i][seq:]
            yre = re * hre - jnp.where(first, 0.0, im * him)
            yim = jnp.where(first, im * him, re * him + im * hre)
            prod.append(jnp.concatenate([yre, yim], axis=0).astype(BF16))
        y = [jnp.dot(inv_ref[...], prod[i], preferred_element_type=F32) for i in par]
        z = [gates[i][o] * (y[i] + z[i] * bias_ref[o:o + 1, :]) for i in par]
    for i in par:
        o_ref[rs[i], :] = z[i]


def _hyena(proj, conv_w, conv_b, spec, bias, prev_out, *, seq, n_seq, row0):
    g = HY_GROUP
    n_g = HY_WIDTH // g
    n_par = 2
    blk_rows = n_par * seq
    blk0 = row0 // blk_rows
    col0 = 3 * MIX_HALF // g
    fwd, inv = _dft_tables(seq)
    conv_b = conv_b.reshape(1, -1)

    def pcol(part):
        return pl.BlockSpec((blk_rows, g), lambda c, b: (blk0 + b, col0 + part * n_g + c))

    def wcol(rows_, part):
        return pl.BlockSpec((rows_, g), lambda c, b: (0, part * n_g + c))

    in_specs = [pcol(0), pcol(1), pcol(2), wcol(3, 0), wcol(3, 1), wcol(3, 2),
                wcol(1, 0), wcol(1, 1), wcol(1, 2),
                pl.BlockSpec((2, 2 * seq, g), lambda c, b: (0, 0, c)),
                pl.BlockSpec((2, g), lambda c, b: (0, c)),
                pl.BlockSpec((2 * seq, seq), lambda c, b: (0, 0), pipeline_mode=pl.Buffered(1)),
                pl.BlockSpec((seq, 2 * seq), lambda c, b: (0, 0), pipeline_mode=pl.Buffered(1))]
    args = [proj, proj, proj, conv_w, conv_w, conv_w, conv_b, conv_b, conv_b, spec, bias,
            jnp.asarray(fwd).astype(BF16), jnp.asarray(inv).astype(BF16)]
    aliases = {}
    if prev_out is not None:
        in_specs.append(pl.BlockSpec(memory_space=pl.ANY))
        args.append(prev_out)
        aliases = {len(args) - 1: 0}

    def body(*refs):
        refs = list(refs)
        if prev_out is not None:
            del refs[len(args) - 1]
        _hy_kernel(*refs, seq=seq, n_par=n_par)

    return pl.pallas_call(
        body,
        grid=(n_g, n_seq // n_par),
        in_specs=in_specs,
        out_specs=pl.BlockSpec((blk_rows, g), lambda c, b: (blk0 + b, c)),
        out_shape=jax.ShapeDtypeStruct((T_ALL, HY_WIDTH), F32),
        input_output_aliases=aliases,
        compiler_params=_params(("arbitrary", "arbitrary")),
        name="hyena_%d" % seq,
    )(*args)


def _softmax_av(s, v, sink):
    m = jnp.max(s, axis=-1, keepdims=True)
    if sink is not None:
        m = jnp.maximum(m, sink)
    e = jnp.exp(s - m)
    den = jnp.sum(e, axis=-1, keepdims=True)
    if sink is not None:
        den = den + jnp.exp(sink - m)
    return _dot(e, v) / den


def _attn_ctx_kernel(p_ref, qn_ref, kn_ref, sink_ref, o_ref, kc_ref, vc_ref, kd_ref, vd_ref):
    hd = HEAD_DIM
    scale = hd ** -0.5
    kv_w = KV_HEADS * hd
    q_w = KV_HEADS * Q_GROUP * hd
    vc_ref[0] = p_ref[:, q_w + kv_w:q_w + 2 * kv_w]
    base_d = q_w + 2 * kv_w
    kd_ref[0] = p_ref[:, base_d + q_w:base_d + q_w + kv_w]
    vd_ref[0] = p_ref[:, base_d + q_w + kv_w:base_d + q_w + 2 * kv_w]
    for kv in range(KV_HEADS):
        ks = slice(kv * hd, (kv + 1) * hd)
        k = _rms(p_ref[:, q_w + kv * hd:q_w + (kv + 1) * hd], kn_ref[...])
        kc_ref[0, :, ks] = k
        v = p_ref[:, q_w + kv_w + kv * hd:q_w + kv_w + (kv + 1) * hd]
        for g in range(Q_GROUP):
            h = kv * Q_GROUP + g
            q = _rms(p_ref[:, h * hd:(h + 1) * hd], qn_ref[...]) * scale
            o_ref[:, h * hd:(h + 1) * hd] = _softmax_av(_dot_nt(q, k), v, None)
        k = p_ref[:, base_d + q_w + kv * hd:base_d + q_w + (kv + 1) * hd]
        v = p_ref[:, base_d + q_w + kv_w + kv * hd:base_d + q_w + kv_w + (kv + 1) * hd]
        for g in range(Q_GROUP):
            h = kv * Q_GROUP + g
            q = p_ref[:, base_d + h * hd:base_d + (h + 1) * hd] * scale
            o_ref[:, q_w + h * hd:q_w + (h + 1) * hd] = _softmax_av(
                _dot_nt(q, k), v, sink_ref[0:1, h:h + 1])


def _attn_ctx(proj, q_norm, k_norm, sink):
    kv_w = KV_HEADS * HEAD_DIM
    cache = jax.ShapeDtypeStruct((BATCH, SEQ, kv_w), F32)
    cache_spec = pl.BlockSpec((1, SEQ, kv_w), lambda b: (b, 0, 0))
    return pl.pallas_call(
        _attn_ctx_kernel,
        grid=(BATCH,),
        in_specs=[
            pl.BlockSpec((SEQ, ODD_COLS), lambda b: (b, 0)),
            pl.BlockSpec((1, HEAD_DIM), lambda b: (0, 0)),
            pl.BlockSpec((1, HEAD_DIM), lambda b: (0, 0)),
            pl.BlockSpec((1, LANES), lambda b: (0, 0)),
        ],
        out_specs=[pl.BlockSpec((SEQ, D_MODEL), lambda b: (b, 0)),
                   cache_spec, cache_spec, cache_spec, cache_spec],
        out_shape=[jax.ShapeDtypeStruct((T_ALL, D_MODEL), F32), cache, cache, cache, cache],
        compiler_params=_params(("arbitrary",)),
        name="attn_ctx",
    )(proj, q_norm.reshape(1, -1), k_norm.reshape(1, -1), sink)


def _rope(x, cos, sin_signed):
    lane = lax.broadcasted_iota(jnp.int32, (1, HEAD_DIM), 1)
    lo = (lane % 64) < 32
    partner = jnp.where(lo, pltpu.roll(x, HEAD_DIM - 32, 1), pltpu.roll(x, 32, 1))
    return x * cos + partner * sin_signed


def _attn_lat_kernel(p_ref, ckc_ref, cvc_ref, ckd_ref, cvd_ref, qn_ref, kn_ref, sink_ref,
                     cos_ref, sin_ref, o_ref, k_s, v_s, q_s, *, q_blk):
    hd = HEAD_DIM
    scale = hd ** -0.5
    seq = DEC_SEQ
    kv_w = KV_HEADS * hd
    q_w = KV_HEADS * Q_GROUP * hd
    n_keys = PAST_LEN + seq
    n_blk = seq // q_blk
    cos = cos_ref[...]
    sin = sin_ref[...]
    kpos = lax.broadcasted_iota(jnp.int32, (1, n_keys), 1) - PAST_LEN
    qrow = lax.broadcasted_iota(jnp.int32, (Q_GROUP * q_blk, 1), 0) % q_blk
    for mixer in range(2):
        base = mixer * (q_w + 2 * kv_w)
        ck_ref, cv_ref = (ckc_ref, cvc_ref) if mixer == 0 else (ckd_ref, cvd_ref)
        for kv in range(KV_HEADS):
            ks = slice(kv * hd, (kv + 1) * hd)
            k = p_ref[:, base + q_w + kv * hd:base + q_w + (kv + 1) * hd]
            if mixer == 0:
                k = _rms(k, kn_ref[...])
            k_s[:PAST_LEN, :] = ck_ref[0, :, ks].astype(BF16)
            k_s[PAST_LEN:, :] = _rope(k, cos, sin).astype(BF16)
            v_s[:PAST_LEN, :] = cv_ref[0, :, ks].astype(BF16)
            v_s[PAST_LEN:, :] = p_ref[:, base + q_w + kv_w + kv * hd:
                                      base + q_w + kv_w + (kv + 1) * hd].astype(BF16)
            for g in range(Q_GROUP):
                h = kv * Q_GROUP + g
                q = p_ref[:, base + h * hd:base + (h + 1) * hd]
                if mixer == 0:
                    q = _rms(q, qn_ref[...])
                q = _rope(q, cos, sin) * scale
                for b in range(n_blk):
                    q_s[b, g * q_blk:(g + 1) * q_blk, :] = q[b * q_blk:(b + 1) * q_blk].astype(BF16)

            def blk(b, carry):
                s = lax.dot_general(q_s[b], k_s[...], (((1,), (1,)), ((), ())),
                                    preferred_element_type=F32)
                if mixer == 1:
                    qpos = b * q_blk + qrow
                    ok = (kpos < 0) | (jnp.abs(kpos - qpos) <= WINDOW)
                    s = jnp.where(ok, s, NEG_BIG)
                r0 = pl.multiple_of(b * q_blk, q_blk)
                for g in range(Q_GROUP):
                    h = kv * Q_GROUP + g
                    sink = sink_ref[0:1, h:h + 1] if mixer == 1 else None
                    o = _softmax_av(s[g * q_blk:(g + 1) * q_blk], v_s[...], sink)
                    o_ref[pl.ds(r0, q_blk), mixer * q_w + h * hd:mixer * q_w + (h + 1) * hd] = o
                return carry

            lax.fori_loop(0, n_blk, blk, 0)


def _attn_lat(proj, ck_c, cv_c, ck_d, cv_d, q_norm, k_norm, sink, cos, sin, prev_out):
    kv_w = KV_HEADS * HEAD_DIM
    q_blk = 256
    blk0 = T_CTX // DEC_SEQ
    cache_spec = pl.BlockSpec((1, PAST_LEN, kv_w), lambda b: (b, 0, 0))
    row = lambda n: pl.BlockSpec((1, n), lambda b: (0, 0))
    tab = pl.BlockSpec((DEC_SEQ, HEAD_DIM), lambda b: (0, 0))

    def body(*refs):
        refs = list(refs)
        del refs[10]
        _attn_lat_kernel(*refs, q_blk=q_blk)

    return pl.pallas_call(
        body,
        grid=(DEC_BATCH,),
        in_specs=[pl.BlockSpec((DEC_SEQ, ODD_COLS), lambda b: (blk0 + b, 0)),
                  cache_spec, cache_spec, cache_spec, cache_spec,
                  row(HEAD_DIM), row(HEAD_DIM), row(LANES), tab, tab,
                  pl.BlockSpec(memory_space=pl.ANY)],
        out_specs=pl.BlockSpec((DEC_SEQ, D_MODEL), lambda b: (blk0 + b, 0)),
        out_shape=jax.ShapeDtypeStruct((T_ALL, D_MODEL), F32),
        scratch_shapes=[pltpu.VMEM((PAST_LEN + DEC_SEQ, HEAD_DIM), BF16),
                        pltpu.VMEM((PAST_LEN + DEC_SEQ, HEAD_DIM), BF16),
                        pltpu.VMEM((DEC_SEQ // q_blk, Q_GROUP * q_blk, HEAD_DIM), BF16)],
        input_output_aliases={10: 0},
        compiler_params=_params(("arbitrary",)),
        name="attn_lat",
    )(proj, ck_c, cv_c, ck_d, cv_d, q_norm.reshape(1, -1), k_norm.reshape(1, -1), sink, cos, sin,
      prev_out)


def _rope_tables(seq):
    rows = seq // GRID_W
    row = jnp.repeat(jnp.arange(rows), GRID_W).astype(F32)
    col = (jnp.arange(rows * GRID_W) % GRID_W).astype(F32)
    half = HEAD_DIM // 2
    inv = ROPE_THETA ** (-jnp.arange(0, half, 2, dtype=F32) / half)
    ar = row[:, None] * inv
    ac = col[:, None] * inv
    cos = jnp.concatenate([jnp.cos(ar), jnp.cos(ar), jnp.cos(ac), jnp.cos(ac)], axis=-1)
    sin = jnp.concatenate([-jnp.sin(ar), jnp.sin(ar), -jnp.sin(ac), jnp.sin(ac)], axis=-1)
    return cos, sin


def _ffn_kernel(*refs, n_mix, tile, final):
    refs = list(refs)
    x_ref = refs[0]
    mix_refs = refs[1:1 + n_mix]
    wo_refs = refs[1 + n_mix:1 + 2 * n_mix]
    pos = 1 + 2 * n_mix
    mod_ref, gain_ref, wup_ref, cw_ref, cb_ref, wdn_ref = refs[pos:pos + 6]
    pos += 6
    fin_ref = None
    if final:
        fin_ref = refs[pos]
        pos += 1
    o_ref = refs[pos]
    h_s, acc_s = refs[pos + 1:]

    m = mod_ref[0]
    mixed = None
    for mr, wr in zip(mix_refs, wo_refs):
        part = jnp.dot(mr[...].astype(BF16), wr[...], preferred_element_type=F32)
        mixed = part if mixed is None else mixed + part
    x1 = x_ref[...] + m[2:3] * mixed
    o_ref[...] = x1
    h_s[...] = (_rms(x1, gain_ref[...]) * (1.0 + m[4:5]) + m[3:4]).astype(BF16)
    acc_s[...] = jnp.zeros_like(acc_s)

    seq_mask = jnp.where(pl.program_id(0) < T_CTX // tile, SEQ - 1, DEC_SEQ - 1)
    pos_in_seq = lax.broadcasted_iota(jnp.int32, (tile, 1), 0) & seq_mask
    first = pos_in_seq == 0
    last = pos_in_seq == seq_mask

    def up(c):
        cols_a = slice(c * FF_CHUNK, (c + 1) * FF_CHUNK)
        cols_b = slice(D_FF + c * FF_CHUNK, D_FF + (c + 1) * FF_CHUNK)
        h = h_s[...]
        return (jnp.dot(h, wup_ref[:, cols_a], preferred_element_type=F32),
                jnp.dot(h, wup_ref[:, cols_b], preferred_element_type=F32))

    ab = up(0)
    for c in range(N_FF_CHUNKS):
        nxt = up(c + 1) if c + 1 < N_FF_CHUNKS else None
        cols = slice(c * FF_CHUNK, (c + 1) * FF_CHUNK)
        a = _conv3(ab[0], cw_ref[:, cols], first, last, tile) + cb_ref[:, cols]
        hid = (_silu(a) * ab[1]).astype(BF16)
        acc_s[...] += jnp.dot(hid, wdn_ref[cols, :], preferred_element_type=F32)
        ab = nxt
    x2 = o_ref[...] + m[5:6] * acc_s[...]
    if final:
        x2 = _rms(x2, fin_ref[...])
    o_ref[...] = x2


def _ffn(x, mixes, w_outs, mod, gain, w_up, conv_w, conv_b, w_down, final_gain):
    tile = 1024
    n_mix = len(mixes)
    final = final_gain is not None
    const = lambda shape: pl.BlockSpec(shape, lambda i: (0,) * len(shape),
                                       pipeline_mode=pl.Buffered(1))
    in_specs = [pl.BlockSpec((tile, D_MODEL), lambda i: (i, 0))]
    in_specs += [pl.BlockSpec((tile, mx.shape[1]), lambda i: (i, 0)) for mx in mixes]
    in_specs += [const(w.shape) for w in w_outs]
    in_specs += [
        pl.BlockSpec((1, 6, D_MODEL), lambda i: (_mod_row(i, tile), 0, 0)),
        const((1, D_MODEL)),
        const((D_MODEL, 2 * D_FF)),
        const((3, D_FF)),
        const((1, D_FF)),
        const((D_FF, D_MODEL)),
    ]
    args = [x, *mixes, *w_outs, mod, gain.reshape(1, D_MODEL), w_up, conv_w, conv_b, w_down]
    if final:
        in_specs.append(const((1, D_MODEL)))
        args.append(final_gain.reshape(1, D_MODEL))
    return pl.pallas_call(
        functools.partial(_ffn_kernel, n_mix=n_mix, tile=tile, final=final),
        grid=(T_ALL // tile,),
        in_specs=in_specs,
        out_specs=pl.BlockSpec((tile, D_MODEL), lambda i: (i, 0)),
        out_shape=jax.ShapeDtypeStruct((T_ALL, D_MODEL), F32),
        scratch_shapes=[pltpu.VMEM((tile, D_MODEL), BF16), pltpu.VMEM((tile, D_MODEL), F32)],
        compiler_params=_params(("arbitrary",)),
        name="outproj_ffn",
    )(*args)


def _even_weight(w):
    ha = MIX_HALF
    n_ba = 4 * DN_HEADS
    main = jnp.concatenate([w[:, :3 * ha], w[:, 4 * ha + n_ba:], w[:, 3 * ha:4 * ha]], axis=1)
    ba = jnp.pad(w[:, 4 * ha:4 * ha + n_ba], ((0, 0), (0, LANES - n_ba)))
    return jnp.concatenate([main, ba], axis=1).astype(BF16)


def kernel(x_prompt, x_sample, state_dn, cache_k_c, cache_v_c, cache_k_d, cache_v_d, c, c_ctx,
           final_norm, w_mod, b_mod, norm_mix, norm_ffn, w_out, ffn_w_up, ffn_conv_w, ffn_conv_b,
           ffn_w_down, ev_w_in, dn_conv_w, dn_a_log, dn_dt_bias, dn_norm, hy_conv_w, hy_conv_b,
           hy_w1, hy_b1, hy_freq1, hy_w2, hy_b2, hy_freq2, hy_w3, hy_bias, od_w_in, c_q_norm,
           c_k_norm, d_sink):
    x = jnp.concatenate([x_prompt.reshape(T_CTX, D_MODEL), x_sample.reshape(T_LAT, D_MODEL)], axis=0)
    cvec = jnp.concatenate([c_ctx[None, :], c, jnp.zeros((MOD_ROWS - 1 - DEC_BATCH, D_MODEL), F32)], axis=0)
    mods = _modulation(cvec, w_mod, b_mod).reshape(DEPTH, MOD_ROWS, 6, D_MODEL)
    cos, sin = _rope_tables(DEC_SEQ)
    kv_w = KV_HEADS * HEAD_DIM

    states, k_c, v_c, k_d, v_d = [], [], [], [], []
    for i in range(DEPTH):
        j = i // 2
        mod = mods[i]
        if i % 2 == 0:
            proj = _inproj(x, mod, norm_mix[i], _even_weight(ev_w_in[j]))
            par = jnp.zeros((8, LANES), F32)
            par = par.at[0, 8:8 + 2 * DN_HEADS].set(dn_a_log[j].reshape(-1))
            par = par.at[1, 8:8 + 2 * DN_HEADS].set(dn_dt_bias[j].reshape(-1))
            norm = dn_norm[j].reshape(1, HEAD_DIM)
            o_mix, s_fin = _deltanet(proj, dn_conv_w[j], par, norm, None, None,
                                     seq=SEQ, n_seq=BATCH, row0=0)
            (o_mix,) = _deltanet(proj, dn_conv_w[j], par, norm, state_dn[:, j], o_mix,
                                 seq=DEC_SEQ, n_seq=DEC_BATCH, row0=T_CTX)
            states.append(s_fin)
            filt = (hy_w1[j], hy_b1[j], hy_freq1[j], hy_w2[j], hy_b2[j], hy_freq2[j], hy_w3[j])
            y_mix = _hyena(proj, hy_conv_w[j], hy_conv_b[j], _hyena_filters(SEQ, *filt), hy_bias[j],
                           None, seq=SEQ, n_seq=BATCH, row0=0)
            y_mix = _hyena(proj, hy_conv_w[j], hy_conv_b[j], _hyena_filters(DEC_SEQ, *filt),
                           hy_bias[j], y_mix, seq=DEC_SEQ, n_seq=DEC_BATCH, row0=T_CTX)
            mixes = [o_mix, y_mix]
            wo = w_out[i].astype(BF16)
            w_outs = [wo[:MIX_HALF], wo[MIX_HALF:]]
        else:
            proj = _inproj(x, mod, norm_mix[i], od_w_in[j].astype(BF16))
            sink = jnp.pad(d_sink[j].reshape(1, -1), ((0, 0), (0, LANES - d_sink.shape[1])))
            mix, kc, vc, kd, vd = _attn_ctx(proj, c_q_norm[j], c_k_norm[j], sink)
            k_c.append(kc)
            v_c.append(vc)
            k_d.append(kd)
            v_d.append(vd)
            flat = lambda t: t[:, j].reshape(DEC_BATCH, PAST_LEN, kv_w)
            mix = _attn_lat(proj, flat(cache_k_c), flat(cache_v_c), flat(cache_k_d), flat(cache_v_d),
                            c_q_norm[j], c_k_norm[j], sink, cos, sin, mix)
            mixes = [mix]
            w_outs = [w_out[i].astype(BF16)]
        x = _ffn(x, mixes, w_outs, mod, norm_ffn[i], ffn_w_up[i].astype(BF16), ffn_conv_w[i],
                 ffn_conv_b[i].reshape(1, D_FF), ffn_w_down[i].astype(BF16),
                 final_norm if i == DEPTH - 1 else None)

    y_prompt = x[:T_CTX].reshape(BATCH, SEQ, D_MODEL)
    y_sample = x[T_CTX:].reshape(DEC_BATCH, DEC_SEQ, D_MODEL)
    heads = lambda ts: jnp.stack(ts, axis=1).reshape(BATCH, len(ts), SEQ, KV_HEADS, HEAD_DIM)
    return (y_prompt, y_sample, jnp.stack(states, axis=1), heads(k_c), heads(v_c), heads(k_d),
            heads(v_d))
```

```python
import functools
import math

import jax
import jax.numpy as jnp
import numpy as np
from jax import lax
from jax.experimental import pallas as pl
from jax.experimental.pallas import tpu as pltpu

F32 = jnp.float32
BF16 = jnp.bfloat16

D_MODEL = 1024
BATCH = 16
SEQ = 256
DEPTH = 4
DEC_BATCH = 8
DEC_SEQ = 1024
PAST_LEN = 256
GRID_W = 64
HEAD_DIM = 128
MIX_HALF = D_MODEL // 2
DN_HEADS = MIX_HALF // HEAD_DIM
DN_CHUNK = 64
HY_WIDTH = MIX_HALF
HY_BANDS = 16
HY_FFN = 64
HY_DECAY_PCT_SHORT = 0.3
HY_DECAY_PCT_LONG = 1.5
HY_TARGET = 1e-2
KV_HEADS = 2
Q_GROUP = 2
WINDOW = 128
ROPE_THETA = 10000.0
D_FF = 2816
EPS = 1e-6
NEG_BIG = -1e30

T_CTX = BATCH * SEQ
T_LAT = DEC_BATCH * DEC_SEQ
T_ALL = T_CTX + T_LAT
MOD_ROWS = 16
LANES = 128
VMEM_LIMIT = 56 * 1024 * 1024
FF_CHUNK = 256
N_FF_CHUNKS = D_FF // FF_CHUNK
HY_GROUP = 256
EVEN_COLS = 3 * MIX_HALF + 3 * HY_WIDTH + MIX_HALF + LANES
ODD_COLS = 2048

HIGHEST = lax.Precision.HIGHEST


def _params(sem):
    return pltpu.CompilerParams(dimension_semantics=sem, vmem_limit_bytes=VMEM_LIMIT)


def _sigmoid(x):
    return 1.0 / (1.0 + jnp.exp(-x))


def _silu(x):
    return x * _sigmoid(x)


def _softplus(x):
    return jnp.maximum(x, 0.0) + jnp.log1p(jnp.exp(-jnp.abs(x)))


def _dot(a, b):
    return jnp.dot(a.astype(BF16), b.astype(BF16), preferred_element_type=F32)


def _dot_nt(a, b):
    return lax.dot_general(a.astype(BF16), b.astype(BF16), (((1,), (1,)), ((), ())),
                           preferred_element_type=F32)


def _dot_tn(a, b):
    return lax.dot_general(a.astype(BF16), b.astype(BF16), (((0,), (0,)), ((), ())),
                           preferred_element_type=F32)


def _dot_split(a, b):
    a_hi = a.astype(BF16)
    b_hi = b.astype(BF16)
    a_lo = (a - a_hi.astype(F32)).astype(BF16)
    b_lo = (b - b_hi.astype(F32)).astype(BF16)
    dot = functools.partial(jnp.dot, preferred_element_type=F32)
    return dot(a_hi, b_hi) + dot(a_hi, b_lo) + dot(a_lo, b_hi)


def _dot_hi(a, b):
    return jnp.dot(a, b, preferred_element_type=F32, precision=HIGHEST)


def _rms(x, gain):
    return x * lax.rsqrt(jnp.mean(x * x, axis=-1, keepdims=True) + EPS) * gain


def _mod_row(i, tile):
    n_ctx = T_CTX // tile
    return jnp.where(i < n_ctx, 0, 1 + (i - n_ctx) // (DEC_SEQ // tile))


def _mod_kernel(c_ref, w_ref, b_ref, o_ref):
    o_ref[0] = _dot(_silu(c_ref[...]), w_ref[0]) + b_ref[0]


def _modulation(cvec, w_mod, b_mod):
    n_tile = 1536
    return pl.pallas_call(
        _mod_kernel,
        grid=(DEPTH, 6 * D_MODEL // n_tile),
        in_specs=[
            pl.BlockSpec((MOD_ROWS, D_MODEL), lambda l, n: (0, 0)),
            pl.BlockSpec((1, D_MODEL, n_tile), lambda l, n: (l, 0, n)),
            pl.BlockSpec((1, 1, n_tile), lambda l, n: (l, 0, n)),
        ],
        out_specs=pl.BlockSpec((1, MOD_ROWS, n_tile), lambda l, n: (l, 0, n)),
        out_shape=jax.ShapeDtypeStruct((DEPTH, MOD_ROWS, 6 * D_MODEL), F32),
        compiler_params=_params(("arbitrary", "arbitrary")),
        name="modulation",
    )(cvec, w_mod, b_mod.reshape(DEPTH, 1, 6 * D_MODEL))


def _inproj_kernel(x_ref, mod_ref, gain_ref, w_ref, o_ref, *, n_chunk):
    m = mod_ref[0]
    h = _rms(x_ref[...], gain_ref[...]) * (1.0 + m[1:2]) + m[0:1]
    hb = h.astype(BF16)
    n_cols = o_ref.shape[1]
    for c0 in range(0, n_cols, n_chunk):
        c1 = min(c0 + n_chunk, n_cols)
        o_ref[:, c0:c1] = jnp.dot(hb, w_ref[:, c0:c1], preferred_element_type=F32)


def _inproj(x, mod, gain, w):
    tile = 512
    n_cols = w.shape[1]
    return pl.pallas_call(
        functools.partial(_inproj_kernel, n_chunk=512),
        grid=(T_ALL // tile,),
        in_specs=[
            pl.BlockSpec((tile, D_MODEL), lambda i: (i, 0)),
            pl.BlockSpec((1, 6, D_MODEL), lambda i: (_mod_row(i, tile), 0, 0)),
            pl.BlockSpec((1, D_MODEL), lambda i: (0, 0)),
            pl.BlockSpec((D_MODEL, n_cols), lambda i: (0, 0)),
        ],
        out_specs=pl.BlockSpec((tile, n_cols), lambda i: (i, 0)),
        out_shape=jax.ShapeDtypeStruct((T_ALL, n_cols), F32),
        compiler_params=_params(("arbitrary",)),
        name="inproj",
    )(x, mod, gain.reshape(1, D_MODEL), w)


def _conv3(x, w, first, last, seq_rows):
    xp = jnp.where(first, 0.0, pltpu.roll(x, 1, 0))
    xn = jnp.where(last, 0.0, pltpu.roll(x, seq_rows - 1, 0))
    return xp * w[0:1] + x * w[1:2] + xn * w[2:3]


def _dn_kernel(*refs, seq, use_s0, write_state):
    refs = list(refs)
    q_ref, k_ref, v_ref, z_ref, ba_ref, cwq_ref, cwk_ref, cwv_ref, par_ref, norm_ref = refs[:10]
    pos = 10
    s0_ref = None
    if use_s0:
        s0_ref = refs[pos]
        pos += 1
    o_ref = refs[pos]
    pos += 1
    sfin_ref = None
    if write_state:
        sfin_ref = refs[pos]
        pos += 1
    q_s, k_s, v_s, of_s, ob_s, g_s, gr_s, beta_s, st_s = refs[pos:]

    C = DN_CHUNK
    n_chunks = seq // C
    hd = HEAD_DIM

    rows = lax.broadcasted_iota(jnp.int32, (seq, 1), 0)
    first = rows == 0
    last = rows == seq - 1
    for h in range(DN_HEADS):
        hs = slice(h * hd, (h + 1) * hd)
        q = _silu(_conv3(q_ref[:, hs], cwq_ref[:, hs], first, last, seq))
        k = _silu(_conv3(k_ref[:, hs], cwk_ref[:, hs], first, last, seq))
        v = _silu(_conv3(v_ref[:, hs], cwv_ref[:, hs], first, last, seq))
        q_s[:, hs] = q * lax.rsqrt(jnp.sum(q * q, axis=-1, keepdims=True) + EPS) * (hd ** -0.5)
        k_s[:, hs] = k * lax.rsqrt(jnp.sum(k * k, axis=-1, keepdims=True) + EPS)
        v_s[:, hs] = v

    ba = ba_ref[...]
    beta_s[...] = _sigmoid(ba)
    g = -jnp.exp(par_ref[0:1]) * _softplus(ba + par_ref[1:2])
    lane = lax.broadcasted_iota(jnp.int32, (1, LANES), 1)
    fwd_lane = lane < 8 + DN_HEADS
    ci = lax.broadcasted_iota(jnp.int32, (C, C), 0)
    cj = lax.broadcasted_iota(jnp.int32, (C, C), 1)
    tril1 = (ci >= cj).astype(F32)
    triu1 = (ci <= cj).astype(F32)
    for n in range(n_chunks):
        gn = g[n * C:(n + 1) * C]
        gc = jnp.where(fwd_lane, _dot_hi(tril1, gn), _dot_hi(triu1, gn))
        g_s[n * C:(n + 1) * C] = gc
        gr_s[n] = gc.T

    if use_s0:
        for d in range(2):
            for h in range(DN_HEADS):
                st_s[d * DN_HEADS + h] = s0_ref[0, d, h]
    else:
        st_s[...] = jnp.zeros_like(st_s)

    def step(t, carry):
        loaded = []
        for d in range(2):
            n = t if d == 0 else n_chunks - 1 - t
            r0 = pl.multiple_of(n * C, C)
            r_last = r0 + (C - 1 if d == 0 else 0)
            for h in range(DN_HEADS):
                p = d * DN_HEADS + h
                hs = slice(h * hd, (h + 1) * hd)
                loaded.append((
                    q_s[pl.ds(r0, C), hs], k_s[pl.ds(r0, C), hs], v_s[pl.ds(r0, C), hs],
                    g_s[pl.ds(r0, C), 8 + p:9 + p],
                    gr_s[n, 8 + p:9 + p, :],
                    beta_s[pl.ds(r0, C), p:p + 1],
                    g_s[pl.ds(r_last, 1), 8 + p:9 + p],
                    st_s[p], r0))
        n_p = len(loaded)
        qn, kn, vn, gcol, grow, bcol, glast, s, r0 = (list(col) for col in zip(*loaded))
        incl = [(ci >= cj) if p < DN_HEADS else (ci <= cj) for p in range(n_p)]
        strict = [(ci > cj) if p < DN_HEADS else (ci < cj) for p in range(n_p)]
        decay = [jnp.exp(jnp.where(incl[p], gcol[p] - grow[p], -jnp.inf)) for p in range(n_p)]
        kb = [kn[p] * bcol[p] for p in range(n_p)]
        y = [jnp.where(strict[p], -(_dot_nt(kb[p], kn[p]) * decay[p]), 0.0) for p in range(n_p)]
        attn = [_dot_nt(qn[p], kn[p]) * decay[p] for p in range(n_p)]
        q = [_dot_split(y[p], y[p]) for p in range(n_p)]
        for _ in range(4):
            both = [_dot_split(jnp.concatenate([q[p], y[p]], axis=0), q[p]) for p in range(n_p)]
            y = [y[p] + q[p] + both[p][C:] for p in range(n_p)]
            q = [both[p][:C] for p in range(n_p)]
        yq = [_dot_split(y[p], q[p]) for p in range(n_p)]
        y = [y[p] + q[p] + yq[p] for p in range(n_p)]
        egc = [jnp.exp(gcol[p]) for p in range(n_p)]
        rhs = [jnp.concatenate([vn[p] * bcol[p], kb[p] * egc[p]], axis=1) for p in range(n_p)]
        sol = [rhs[p] + _dot_split(y[p], rhs[p]) for p in range(n_p)]
        v_new = [sol[p][:, :hd] - _dot(sol[p][:, hd:], s[p]) for p in range(n_p)]
        o_old = [_dot(qn[p] * egc[p], s[p]) for p in range(n_p)]
        o = [o_old[p] + _dot(attn[p], v_new[p]) for p in range(n_p)]
        s_new = [s[p] * jnp.exp(glast[p]) + _dot_tn(kn[p] * jnp.exp(glast[p] - gcol[p]), v_new[p])
                 for p in range(n_p)]
        for p in range(n_p):
            d, h = divmod(p, DN_HEADS)
            st_s[p] = s_new[p]
            (of_s if d == 0 else ob_s)[pl.ds(r0[p], C), h * hd:(h + 1) * hd] = o[p]
        return carry

    lax.fori_loop(0, n_chunks, step, 0)

    for h in range(DN_HEADS):
        hs = slice(h * hd, (h + 1) * hd)
        o_ref[:, hs] = _rms(of_s[:, hs] + ob_s[:, hs], norm_ref[...]) * _silu(z_ref[:, hs])
    if write_state:
        for d in range(2):
            for h in range(DN_HEADS):
                sfin_ref[0, d, h] = st_s[d * DN_HEADS + h]


def _deltanet(proj, conv_w, par, norm, s0, prev_out, *, seq, n_seq, row0):
    use_s0 = s0 is not None
    write_state = not use_s0
    blk0 = row0 // seq
    half = MIX_HALF
    in_specs = [
        pl.BlockSpec((seq, half), lambda b: (blk0 + b, 0)),
        pl.BlockSpec((seq, half), lambda b: (blk0 + b, 1)),
        pl.BlockSpec((seq, half), lambda b: (blk0 + b, 2)),
        pl.BlockSpec((seq, half), lambda b: (blk0 + b, 6)),
        pl.BlockSpec((seq, LANES), lambda b: (blk0 + b, 28)),
        pl.BlockSpec((3, half), lambda b: (0, 0)),
        pl.BlockSpec((3, half), lambda b: (0, 1)),
        pl.BlockSpec((3, half), lambda b: (0, 2)),
        pl.BlockSpec((8, LANES), lambda b: (0, 0)),
        pl.BlockSpec((1, HEAD_DIM), lambda b: (0, 0)),
    ]
    args = [proj, proj, proj, proj, proj, conv_w, conv_w, conv_w, par, norm]
    if use_s0:
        in_specs.append(pl.BlockSpec((1, 2, DN_HEADS, HEAD_DIM, HEAD_DIM), lambda b: (b, 0, 0, 0, 0)))
        args.append(s0)
    aliases = {}
    if prev_out is not None:
        in_specs.append(pl.BlockSpec(memory_space=pl.ANY))
        args.append(prev_out)
        aliases = {len(args) - 1: 0}
    out_specs = [pl.BlockSpec((seq, half), lambda b: (blk0 + b, 0))]
    out_shape = [jax.ShapeDtypeStruct((T_ALL, half), F32)]
    if write_state:
        out_specs.append(pl.BlockSpec((1, 2, DN_HEADS, HEAD_DIM, HEAD_DIM), lambda b: (b, 0, 0, 0, 0)))
        out_shape.append(jax.ShapeDtypeStruct((n_seq, 2, DN_HEADS, HEAD_DIM, HEAD_DIM), F32))

    def body(*refs):
        refs = list(refs)
        if prev_out is not None:
            n_in = len(args)
            del refs[n_in - 1]
        _dn_kernel(*refs, seq=seq, use_s0=use_s0, write_state=write_state)

    n_chunks = seq // DN_CHUNK
    res = pl.pallas_call(
        body,
        grid=(n_seq,),
        in_specs=in_specs,
        out_specs=out_specs,
        out_shape=out_shape,
        scratch_shapes=[
            pltpu.VMEM((seq, half), F32),
            pltpu.VMEM((seq, half), F32),
            pltpu.VMEM((seq, half), F32),
            pltpu.VMEM((seq, half), F32),
            pltpu.VMEM((seq, half), F32),
            pltpu.VMEM((seq, LANES), F32),
            pltpu.VMEM((n_chunks, LANES, DN_CHUNK), F32),
            pltpu.VMEM((seq, LANES), F32),
            pltpu.VMEM((2 * DN_HEADS, HEAD_DIM, HEAD_DIM), F32),
        ],
        input_output_aliases=aliases,
        compiler_params=_params(("arbitrary",)),
        name="deltanet_%d" % seq,
    )(*args)
    return res


@functools.lru_cache(maxsize=None)
def _dft_tables(seq):
    k = np.arange(seq, dtype=np.int64)[:, None]
    s = np.arange(seq, dtype=np.int64)[None, :]
    ang = np.pi * ((k * s) % (2 * seq)).astype(np.float64) / seq
    cos = np.cos(ang)
    sin = np.sin(ang)
    fwd = np.concatenate([cos, -sin], axis=0)
    fwd[seq, :] = np.where(np.arange(seq) % 2 == 0, 1.0, -1.0)
    wk = np.full((seq,), 2.0)
    wk[0] = 1.0
    inv = np.concatenate([cos.T * wk[None, :], -2.0 * sin.T], axis=1) / (2.0 * seq)
    inv[:, seq] = np.where(np.arange(seq) % 2 == 0, 1.0, -1.0) / (2.0 * seq)
    return fwd.astype(np.float32), inv.astype(np.float32)


def _hy_filter_kernel(feat_ref, w1_ref, b1_ref, f1_ref, w2_ref, b2_ref, f2_ref, w3f_ref, w3b_ref,
                      dec_ref, fwd_ref, o_ref, *, seq):
    z = jnp.sin(f1_ref[...] * (_dot(feat_ref[...], w1_ref[...]) + b1_ref[...]))
    z = jnp.sin(f2_ref[...] * (_dot(z, w2_ref[...]) + b2_ref[...]))
    dec = dec_ref[...]
    h_fwd = _dot(z, w3f_ref[...]) * dec
    h_bwd = _dot(z, w3b_ref[...]) * dec
    rows = lax.broadcasted_iota(jnp.int32, (seq, 1), 0)
    h_bwd = jnp.where(rows == 0, 0.0, h_bwd)
    spec = _dot_split(fwd_ref[...], jnp.concatenate([h_fwd, h_bwd], axis=1))
    g = h_fwd.shape[1]
    srow = lax.broadcasted_iota(jnp.int32, (2 * seq, 1), 0)
    sign = jnp.where(srow <= seq, 1.0, -1.0)
    o_ref[0] = spec[:, :g] + sign * spec[:, g:]


def _hyena_filters(seq, w1, b1, f1, w2, b2, f2, w3):
    t = jnp.linspace(0.0, 1.0, seq, dtype=F32)[:, None]
    w = (2.0 * math.pi / seq) * jnp.arange(seq, dtype=F32)[:, None]
    f = jnp.linspace(1e-4, HY_BANDS - 1, HY_BANDS, dtype=F32)[None, :]
    feats = jnp.concatenate([t, jnp.cos(f * w), -jnp.sin(f * w)], axis=-1)
    n_feat = feats.shape[1]
    feats = jnp.pad(feats, ((0, 0), (0, LANES - n_feat)))
    w1p = jnp.pad(w1, ((0, LANES - n_feat), (0, 0)))
    deltas = jnp.abs(jnp.linspace(math.log(HY_TARGET) / HY_DECAY_PCT_LONG,
                                  math.log(HY_TARGET) / HY_DECAY_PCT_SHORT, HY_WIDTH, dtype=F32))
    dec = jnp.exp(-t * deltas)
    fwd, _ = _dft_tables(seq)
    g = HY_GROUP
    n_g = HY_WIDTH // g
    small = lambda shape: pl.BlockSpec(shape, lambda o, c: (0,) * len(shape))
    return pl.pallas_call(
        functools.partial(_hy_filter_kernel, seq=seq),
        grid=(2, n_g),
        in_specs=[
            small((seq, LANES)), small((LANES, HY_FFN)), small((1, HY_FFN)), small((1, HY_FFN)),
            small((HY_FFN, HY_FFN)), small((1, HY_FFN)), small((1, HY_FFN)),
            pl.BlockSpec((HY_FFN, g), lambda o, c: (0, (2 * o) * n_g + c)),
            pl.BlockSpec((HY_FFN, g), lambda o, c: (0, (2 * o + 1) * n_g + c)),
            pl.BlockSpec((seq, g), lambda o, c: (0, c)),
            small((2 * seq, seq)),
        ],
        out_specs=pl.BlockSpec((1, 2 * seq, g), lambda o, c: (o, 0, c)),
        out_shape=jax.ShapeDtypeStruct((2, 2 * seq, HY_WIDTH), F32),
        compiler_params=_params(("arbitrary", "arbitrary")),
        name="hyena_filters_%d" % seq,
    )(feats, w1p, b1.reshape(1, -1), f1.reshape(1, -1), w2, b2.reshape(1, -1), f2.reshape(1, -1),
      w3, w3, dec, jnp.asarray(fwd))


def _hy_kernel(x1_ref, x2_ref, z_ref, cw1_ref, cw2_ref, cwz_ref, cb1_ref, cb2_ref, cbz_ref,
               spec_ref, bias_ref, fwd_ref, inv_ref, o_ref, *, seq, n_par):
    rows = lax.broadcasted_iota(jnp.int32, (seq, 1), 0)
    first = rows == 0
    last = rows == seq - 1
    par = range(n_par)
    rs = [slice(i * seq, (i + 1) * seq) for i in par]
    gates = [(_conv3(x1_ref[rs[i], :], cw1_ref[...], first, last, seq) + cb1_ref[...],
              _conv3(x2_ref[rs[i], :], cw2_ref[...], first, last, seq) + cb2_ref[...]) for i in par]
    z = [_conv3(z_ref[rs[i], :], cwz_ref[...], first, last, seq) + cbz_ref[...] for i in par]
    for o in range(2):
        zf = [jnp.dot(fwd_ref[...], z[i].astype(BF16), preferred_element_type=F32) for i in par]
        hre, him = spec_ref[o, :seq, :], spec_ref[o, seq:, :]
        prod = []
        for i in par:
            re, im = zf[i][:seq], zf[i][seq:]
            yre = re * hre - jnp.where(first, 0.0, im * him)
            yim = jnp.where(first, im * him, re * him + im * hre)
            prod.append(jnp.concatenate([yre, yim], axis=0).astype(BF16))
        y = [jnp.dot(inv_ref[...], prod[i], preferred_element_type=F32) for i in par]
        z = [gates[i][o] * (y[i] + z[i] * bias_ref[o:o + 1, :]) for i in par]
    for i in par:
        o_ref[rs[i], :] = z[i]


def _hyena(proj, conv_w, conv_b, spec, bias, prev_out, *, seq, n_seq, row0):
    g = HY_GROUP
    n_g = HY_WIDTH // g
    n_par = 2
    blk_rows = n_par * seq
    blk0 = row0 // blk_rows
    col0 = 3 * MIX_HALF // g
    fwd, inv = _dft_tables(seq)
    conv_b = conv_b.reshape(1, -1)

    def pcol(part):
        return pl.BlockSpec((blk_rows, g), lambda c, b: (blk0 + b, col0 + part * n_g + c))

    def wcol(rows_, part):
        return pl.BlockSpec((rows_, g), lambda c, b: (0, part * n_g + c))

    in_specs = [pcol(0), pcol(1), pcol(2), wcol(3, 0), wcol(3, 1), wcol(3, 2),
                wcol(1, 0), wcol(1, 1), wcol(1, 2),
                pl.BlockSpec((2, 2 * seq, g), lambda c, b: (0, 0, c)),
                pl.BlockSpec((2, g), lambda c, b: (0, c)),
                pl.BlockSpec((2 * seq, seq), lambda c, b: (0, 0), pipeline_mode=pl.Buffered(1)),
                pl.BlockSpec((seq, 2 * seq), lambda c, b: (0, 0), pipeline_mode=pl.Buffered(1))]
    args = [proj, proj, proj, conv_w, conv_w, conv_w, conv_b, conv_b, conv_b, spec, bias,
            jnp.asarray(fwd).astype(BF16), jnp.asarray(inv).astype(BF16)]
    aliases = {}
    if prev_out is not None:
        in_specs.append(pl.BlockSpec(memory_space=pl.ANY))
        args.append(prev_out)
        aliases = {len(args) - 1: 0}

    def body(*refs):
        refs = list(refs)
        if prev_out is not None:
            del refs[len(args) - 1]
        _hy_kernel(*refs, seq=seq, n_par=n_par)

    return pl.pallas_call(
        body,
        grid=(n_g, n_seq // n_par),
        in_specs=in_specs,
        out_specs=pl.BlockSpec((blk_rows, g), lambda c, b: (blk0 + b, c)),
        out_shape=jax.ShapeDtypeStruct((T_ALL, HY_WIDTH), F32),
        input_output_aliases=aliases,
        compiler_params=_params(("arbitrary", "arbitrary")),
        name="hyena_%d" % seq,
    )(*args)


def _softmax_av(s, v, sink):
    m = jnp.max(s, axis=-1, keepdims=True)
    if sink is not None:
        m = jnp.maximum(m, sink)
    e = jnp.exp(s - m)
    den = jnp.sum(e, axis=-1, keepdims=True)
    if sink is not None:
        den = den + jnp.exp(sink - m)
    return _dot(e, v) / den


def _attn_ctx_kernel(p_ref, qn_ref, kn_ref, sink_ref, o_ref, kc_ref, vc_ref, kd_ref, vd_ref):
    hd = HEAD_DIM
    scale = hd ** -0.5
    kv_w = KV_HEADS * hd
    q_w = KV_HEADS * Q_GROUP * hd
    vc_ref[0] = p_ref[:, q_w + kv_w:q_w + 2 * kv_w]
    base_d = q_w + 2 * kv_w
    kd_ref[0] = p_ref[:, base_d + q_w:base_d + q_w + kv_w]
    vd_ref[0] = p_ref[:, base_d + q_w + kv_w:base_d + q_w + 2 * kv_w]
    for kv in range(KV_HEADS):
        ks = slice(kv * hd, (kv + 1) * hd)
        k = _rms(p_ref[:, q_w + kv * hd:q_w + (kv + 1) * hd], kn_ref[...])
        kc_ref[0, :, ks] = k
        v = p_ref[:, q_w + kv_w + kv * hd:q_w + kv_w + (kv + 1) * hd]
        for g in range(Q_GROUP):
            h = kv * Q_GROUP + g
            q = _rms(p_ref[:, h * hd:(h + 1) * hd], qn_ref[...]) * scale
            o_ref[:, h * hd:(h + 1) * hd] = _softmax_av(_dot_nt(q, k), v, None)
        k = p_ref[:, base_d + q_w + kv * hd:base_d + q_w + (kv + 1) * hd]
        v = p_ref[:, base_d + q_w + kv_w + kv * hd:base_d + q_w + kv_w + (kv + 1) * hd]
        for g in range(Q_GROUP):
            h = kv * Q_GROUP + g
            q = p_ref[:, base_d + h * hd:base_d + (h + 1) * hd] * scale
            o_ref[:, q_w + h * hd:q_w + (h + 1) * hd] = _softmax_av(
                _dot_nt(q, k), v, sink_ref[0:1, h:h + 1])


def _attn_ctx(proj, q_norm, k_norm, sink):
    kv_w = KV_HEADS * HEAD_DIM
    cache = jax.ShapeDtypeStruct((BATCH, SEQ, kv_w), F32)
    cache_spec = pl.BlockSpec((1, SEQ, kv_w), lambda b: (b, 0, 0))
    return pl.pallas_call(
        _attn_ctx_kernel,
        grid=(BATCH,),
        in_specs=[
            pl.BlockSpec((SEQ, ODD_COLS), lambda b: (b, 0)),
            pl.BlockSpec((1, HEAD_DIM), lambda b: (0, 0)),
            pl.BlockSpec((1, HEAD_DIM), lambda b: (0, 0)),
            pl.BlockSpec((1, LANES), lambda b: (0, 0)),
        ],
        out_specs=[pl.BlockSpec((SEQ, D_MODEL), lambda b: (b, 0)),
                   cache_spec, cache_spec, cache_spec, cache_spec],
        out_shape=[jax.ShapeDtypeStruct((T_ALL, D_MODEL), F32), cache, cache, cache, cache],
        compiler_params=_params(("arbitrary",)),
        name="attn_ctx",
    )(proj, q_norm.reshape(1, -1), k_norm.reshape(1, -1), sink)


def _rope(x, cos, sin_signed):
    lane = lax.broadcasted_iota(jnp.int32, (1, HEAD_DIM), 1)
    lo = (lane % 64) < 32
    partner = jnp.where(lo, pltpu.roll(x, HEAD_DIM - 32, 1), pltpu.roll(x, 32, 1))
    return x * cos + partner * sin_signed


def _attn_lat_kernel(p_ref, ckc_ref, cvc_ref, ckd_ref, cvd_ref, qn_ref, kn_ref, sink_ref,
                     cos_ref, sin_ref, o_ref, k_s, v_s, q_s, *, q_blk):
    hd = HEAD_DIM
    scale = hd ** -0.5
    seq = DEC_SEQ
    kv_w = KV_HEADS * hd
    q_w = KV_HEADS * Q_GROUP * hd
    n_keys = PAST_LEN + seq
    n_blk = seq // q_blk
    cos = cos_ref[...]
    sin = sin_ref[...]
    kpos = lax.broadcasted_iota(jnp.int32, (1, n_keys), 1) - PAST_LEN
    qrow = lax.broadcasted_iota(jnp.int32, (Q_GROUP * q_blk, 1), 0) % q_blk
    for mixer in range(2):
        base = mixer * (q_w + 2 * kv_w)
        ck_ref, cv_ref = (ckc_ref, cvc_ref) if mixer == 0 else (ckd_ref, cvd_ref)
        for kv in range(KV_HEADS):
            ks = slice(kv * hd, (kv + 1) * hd)
            k = p_ref[:, base + q_w + kv * hd:base + q_w + (kv + 1) * hd]
            if mixer == 0:
                k = _rms(k, kn_ref[...])
            k_s[:PAST_LEN, :] = ck_ref[0, :, ks].astype(BF16)
            k_s[PAST_LEN:, :] = _rope(k, cos, sin).astype(BF16)
            v_s[:PAST_LEN, :] = cv_ref[0, :, ks].astype(BF16)
            v_s[PAST_LEN:, :] = p_ref[:, base + q_w + kv_w + kv * hd:
                                      base + q_w + kv_w + (kv + 1) * hd].astype(BF16)
            for g in range(Q_GROUP):
                h = kv * Q_GROUP + g
                q = p_ref[:, base + h * hd:base + (h + 1) * hd]
                if mixer == 0:
                    q = _rms(q, qn_ref[...])
                q = _rope(q, cos, sin) * scale
                for b in range(n_blk):
                    q_s[b, g * q_blk:(g + 1) * q_blk, :] = q[b * q_blk:(b + 1) * q_blk].astype(BF16)

            def blk(i, carry):
                blocks = [2 * i, 2 * i + 1]
                logits = []
                for b in blocks:
                    s = lax.dot_general(q_s[b], k_s[...], (((1,), (1,)), ((), ())),
                                        preferred_element_type=F32)
                    if mixer == 1:
                        qpos = b * q_blk + qrow
                        ok = (kpos < 0) | (jnp.abs(kpos - qpos) <= WINDOW)
                        s = jnp.where(ok, s, NEG_BIG)
                    logits.append(s)
                for b, s in zip(blocks, logits):
                    r0 = pl.multiple_of(b * q_blk, q_blk)
                    for g in range(Q_GROUP):
                        h = kv * Q_GROUP + g
                        sink = sink_ref[0:1, h:h + 1] if mixer == 1 else None
                        o = _softmax_av(s[g * q_blk:(g + 1) * q_blk], v_s[...], sink)
                        o_ref[pl.ds(r0, q_blk),
                              mixer * q_w + h * hd:mixer * q_w + (h + 1) * hd] = o
                return carry

            lax.fori_loop(0, n_blk // 2, blk, 0)


def _attn_lat(proj, ck_c, cv_c, ck_d, cv_d, q_norm, k_norm, sink, cos, sin, prev_out):
    kv_w = KV_HEADS * HEAD_DIM
    q_blk = 256
    blk0 = T_CTX // DEC_SEQ
    cache_spec = pl.BlockSpec((1, PAST_LEN, kv_w), lambda b: (b, 0, 0))
    row = lambda n: pl.BlockSpec((1, n), lambda b: (0, 0))
    tab = pl.BlockSpec((DEC_SEQ, HEAD_DIM), lambda b: (0, 0))

    def body(*refs):
        refs = list(refs)
        del refs[10]
        _attn_lat_kernel(*refs, q_blk=q_blk)

    return pl.pallas_call(
        body,
        grid=(DEC_BATCH,),
        in_specs=[pl.BlockSpec((DEC_SEQ, ODD_COLS), lambda b: (blk0 + b, 0)),
                  cache_spec, cache_spec, cache_spec, cache_spec,
                  row(HEAD_DIM), row(HEAD_DIM), row(LANES), tab, tab,
                  pl.BlockSpec(memory_space=pl.ANY)],
        out_specs=pl.BlockSpec((DEC_SEQ, D_MODEL), lambda b: (blk0 + b, 0)),
        out_shape=jax.ShapeDtypeStruct((T_ALL, D_MODEL), F32),
        scratch_shapes=[pltpu.VMEM((PAST_LEN + DEC_SEQ, HEAD_DIM), BF16),
                        pltpu.VMEM((PAST_LEN + DEC_SEQ, HEAD_DIM), BF16),
                        pltpu.VMEM((DEC_SEQ // q_blk, Q_GROUP * q_blk, HEAD_DIM), BF16)],
        input_output_aliases={10: 0},
        compiler_params=_params(("arbitrary",)),
        name="attn_lat",
    )(proj, ck_c, cv_c, ck_d, cv_d, q_norm.reshape(1, -1), k_norm.reshape(1, -1), sink, cos, sin,
      prev_out)


def _rope_tables(seq):
    rows = seq // GRID_W
    row = jnp.repeat(jnp.arange(rows), GRID_W).astype(F32)
    col = (jnp.arange(rows * GRID_W) % GRID_W).astype(F32)
    half = HEAD_DIM // 2
    inv = ROPE_THETA ** (-jnp.arange(0, half, 2, dtype=F32) / half)
    ar = row[:, None] * inv
    ac = col[:, None] * inv
    cos = jnp.concatenate([jnp.cos(ar), jnp.cos(ar), jnp.cos(ac), jnp.cos(ac)], axis=-1)
    sin = jnp.concatenate([-jnp.sin(ar), jnp.sin(ar), -jnp.sin(ac), jnp.sin(ac)], axis=-1)
    return cos, sin


def _ffn_kernel(*refs, n_mix, tile, final):
    refs = list(refs)
    x_ref = refs[0]
    mix_refs = refs[1:1 + n_mix]
    wo_refs = refs[1 + n_mix:1 + 2 * n_mix]
    pos = 1 + 2 * n_mix
    mod_ref, gain_ref, wup_ref, cw_ref, cb_ref, wdn_ref = refs[pos:pos + 6]
    pos += 6
    fin_ref = None
    if final:
        fin_ref = refs[pos]
        pos += 1
    o_ref = refs[pos]
    h_s, acc_s = refs[pos + 1:]

    m = mod_ref[0]
    mixed = None
    for mr, wr in zip(mix_refs, wo_refs):
        part = jnp.dot(mr[...].astype(BF16), wr[...], preferred_element_type=F32)
        mixed = part if mixed is None else mixed + part
    x1 = x_ref[...] + m[2:3] * mixed
    o_ref[...] = x1
    h_s[...] = (_rms(x1, gain_ref[...]) * (1.0 + m[4:5]) + m[3:4]).astype(BF16)
    acc_s[...] = jnp.zeros_like(acc_s)

    seq_mask = jnp.where(pl.program_id(0) < T_CTX // tile, SEQ - 1, DEC_SEQ - 1)
    pos_in_seq = lax.broadcasted_iota(jnp.int32, (tile, 1), 0) & seq_mask
    first = pos_in_seq == 0
    last = pos_in_seq == seq_mask

    def up(c):
        cols_a = slice(c * FF_CHUNK, (c + 1) * FF_CHUNK)
        cols_b = slice(D_FF + c * FF_CHUNK, D_FF + (c + 1) * FF_CHUNK)
        h = h_s[...]
        return (jnp.dot(h, wup_ref[:, cols_a], preferred_element_type=F32),
                jnp.dot(h, wup_ref[:, cols_b], preferred_element_type=F32))

    ab = up(0)
    for c in range(N_FF_CHUNKS):
        nxt = up(c + 1) if c + 1 < N_FF_CHUNKS else None
        cols = slice(c * FF_CHUNK, (c + 1) * FF_CHUNK)
        a = _conv3(ab[0], cw_ref[:, cols], first, last, tile) + cb_ref[:, cols]
        hid = (_silu(a) * ab[1]).astype(BF16)
        acc_s[...] += jnp.dot(hid, wdn_ref[cols, :], preferred_element_type=F32)
        ab = nxt
    x2 = o_ref[...] + m[5:6] * acc_s[...]
    if final:
        x2 = _rms(x2, fin_ref[...])
    o_ref[...] = x2


def _ffn(x, mixes, w_outs, mod, gain, w_up, conv_w, conv_b, w_down, final_gain):
    tile = 1024
    n_mix = len(mixes)
    final = final_gain is not None
    const = lambda shape: pl.BlockSpec(shape, lambda i: (0,) * len(shape),
                                       pipeline_mode=pl.Buffered(1))
    in_specs = [pl.BlockSpec((tile, D_MODEL), lambda i: (i, 0))]
    in_specs += [pl.BlockSpec((tile, mx.shape[1]), lambda i: (i, 0)) for mx in mixes]
    in_specs += [const(w.shape) for w in w_outs]
    in_specs += [
        pl.BlockSpec((1, 6, D_MODEL), lambda i: (_mod_row(i, tile), 0, 0)),
        const((1, D_MODEL)),
        const((D_MODEL, 2 * D_FF)),
        const((3, D_FF)),
        const((1, D_FF)),
        const((D_FF, D_MODEL)),
    ]
    args = [x, *mixes, *w_outs, mod, gain.reshape(1, D_MODEL), w_up, conv_w, conv_b, w_down]
    if final:
        in_specs.append(const((1, D_MODEL)))
        args.append(final_gain.reshape(1, D_MODEL))
    return pl.pallas_call(
        functools.partial(_ffn_kernel, n_mix=n_mix, tile=tile, final=final),
        grid=(T_ALL // tile,),
        in_specs=in_specs,
        out_specs=pl.BlockSpec((tile, D_MODEL), lambda i: (i, 0)),
        out_shape=jax.ShapeDtypeStruct((T_ALL, D_MODEL), F32),
        scratch_shapes=[pltpu.VMEM((tile, D_MODEL), BF16), pltpu.VMEM((tile, D_MODEL), F32)],
        compiler_params=_params(("arbitrary",)),
        name="outproj_ffn",
    )(*args)


def _even_weight(w):
    ha = MIX_HALF
    n_ba = 4 * DN_HEADS
    main = jnp.concatenate([w[:, :3 * ha], w[:, 4 * ha + n_ba:], w[:, 3 * ha:4 * ha]], axis=1)
    ba = jnp.pad(w[:, 4 * ha:4 * ha + n_ba], ((0, 0), (0, LANES - n_ba)))
    return jnp.concatenate([main, ba], axis=1).astype(BF16)


def kernel(x_prompt, x_sample, state_dn, cache_k_c, cache_v_c, cache_k_d, cache_v_d, c, c_ctx,
           final_norm, w_mod, b_mod, norm_mix, norm_ffn, w_out, ffn_w_up, ffn_conv_w, ffn_conv_b,
           ffn_w_down, ev_w_in, dn_conv_w, dn_a_log, dn_dt_bias, dn_norm, hy_conv_w, hy_conv_b,
           hy_w1, hy_b1, hy_freq1, hy_w2, hy_b2, hy_freq2, hy_w3, hy_bias, od_w_in, c_q_norm,
           c_k_norm, d_sink):
    x = jnp.concatenate([x_prompt.reshape(T_CTX, D_MODEL), x_sample.reshape(T_LAT, D_MODEL)], axis=0)
    cvec = jnp.concatenate([c_ctx[None, :], c, jnp.zeros((MOD_ROWS - 1 - DEC_BATCH, D_MODEL), F32)], axis=0)
    mods = _modulation(cvec, w_mod, b_mod).reshape(DEPTH, MOD_ROWS, 6, D_MODEL)
    cos, sin = _rope_tables(DEC_SEQ)
    kv_w = KV_HEADS * HEAD_DIM

    states, k_c, v_c, k_d, v_d = [], [], [], [], []
    for i in range(DEPTH):
        j = i // 2
        mod = mods[i]
        if i % 2 == 0:
            proj = _inproj(x, mod, norm_mix[i], _even_weight(ev_w_in[j]))
            par = jnp.zeros((8, LANES), F32)
            par = par.at[0, 8:8 + 2 * DN_HEADS].set(dn_a_log[j].reshape(-1))
            par = par.at[1, 8:8 + 2 * DN_HEADS].set(dn_dt_bias[j].reshape(-1))
            norm = dn_norm[j].reshape(1, HEAD_DIM)
            o_mix, s_fin = _deltanet(proj, dn_conv_w[j], par, norm, None, None,
                                     seq=SEQ, n_seq=BATCH, row0=0)
            (o_mix,) = _deltanet(proj, dn_conv_w[j], par, norm, state_dn[:, j], o_mix,
                                 seq=DEC_SEQ, n_seq=DEC_BATCH, row0=T_CTX)
            states.append(s_fin)
            filt = (hy_w1[j], hy_b1[j], hy_freq1[j], hy_w2[j], hy_b2[j], hy_freq2[j], hy_w3[j])
            y_mix = _hyena(proj, hy_conv_w[j], hy_conv_b[j], _hyena_filters(SEQ, *filt), hy_bias[j],
                           None, seq=SEQ, n_seq=BATCH, row0=0)
            y_mix = _hyena(proj, hy_conv_w[j], hy_conv_b[j], _hyena_filters(DEC_SEQ, *filt),
                           hy_bias[j], y_mix, seq=DEC_SEQ, n_seq=DEC_BATCH, row0=T_CTX)
            mixes = [o_mix, y_mix]
            wo = w_out[i].astype(BF16)
            w_outs = [wo[:MIX_HALF], wo[MIX_HALF:]]
        else:
            proj = _inproj(x, mod, norm_mix[i], od_w_in[j].astype(BF16))
            sink = jnp.pad(d_sink[j].reshape(1, -1), ((0, 0), (0, LANES - d_sink.shape[1])))
            mix, kc, vc, kd, vd = _attn_ctx(proj, c_q_norm[j], c_k_norm[j], sink)
            k_c.append(kc)
            v_c.append(vc)
            k_d.append(kd)
            v_d.append(vd)
            flat = lambda t: t[:, j].reshape(DEC_BATCH, PAST_LEN, kv_w)
            mix = _attn_lat(proj, flat(cache_k_c), flat(cache_v_c), flat(cache_k_d), flat(cache_v_d),
                            c_q_norm[j], c_k_norm[j], sink, cos, sin, mix)
            mixes = [mix]
            w_outs = [w_out[i].astype(BF16)]
        x = _ffn(x, mixes, w_outs, mod, norm_ffn[i], ffn_w_up[i].astype(BF16), ffn_conv_w[i],
                 ffn_conv_b[i].reshape(1, D_FF), ffn_w_down[i].astype(BF16),
                 final_norm if i == DEPTH - 1 else None)

    y_prompt = x[:T_CTX].reshape(BATCH, SEQ, D_MODEL)
    y_sample = x[T_CTX:].reshape(DEC_BATCH, DEC_SEQ, D_MODEL)
    heads = lambda ts: jnp.stack(ts, axis=1).reshape(BATCH, len(ts), SEQ, KV_HEADS, HEAD_DIM)
    return (y_prompt, y_sample, jnp.stack(states, axis=1), heads(k_c), heads(v_c), heads(k_d),
            heads(v_d))
```

```python
import functools
import math

import jax
import jax.numpy as jnp
import numpy as np
from jax import lax
from jax.experimental import pallas as pl
from jax.experimental.pallas import tpu as pltpu

F32 = jnp.float32
BF16 = jnp.bfloat16

D_MODEL = 1024
BATCH = 16
SEQ = 256
DEPTH = 4
DEC_BATCH = 8
DEC_SEQ = 1024
PAST_LEN = 256
GRID_W = 64
HEAD_DIM = 128
MIX_HALF = D_MODEL // 2
DN_HEADS = MIX_HALF // HEAD_DIM
DN_CHUNK = 64
HY_WIDTH = MIX_HALF
HY_BANDS = 16
HY_FFN = 64
HY_DECAY_PCT_SHORT = 0.3
HY_DECAY_PCT_LONG = 1.5
HY_TARGET = 1e-2
KV_HEADS = 2
Q_GROUP = 2
WINDOW = 128
ROPE_THETA = 10000.0
D_FF = 2816
EPS = 1e-6
NEG_BIG = -1e30

T_CTX = BATCH * SEQ
T_LAT = DEC_BATCH * DEC_SEQ
T_ALL = T_CTX + T_LAT
MOD_ROWS = 16
LANES = 128
VMEM_LIMIT = 56 * 1024 * 1024
FF_CHUNK = 256
N_FF_CHUNKS = D_FF // FF_CHUNK
HY_GROUP = 256
EVEN_COLS = 3 * MIX_HALF + 3 * HY_WIDTH + MIX_HALF + LANES
ODD_COLS = 2048

HIGHEST = lax.Precision.HIGHEST


def _params(sem):
    return pltpu.CompilerParams(dimension_semantics=sem, vmem_limit_bytes=VMEM_LIMIT)


def _sigmoid(x):
    return 1.0 / (1.0 + jnp.exp(-x))


def _silu(x):
    return x * _sigmoid(x)


def _softplus(x):
    return jnp.maximum(x, 0.0) + jnp.log1p(jnp.exp(-jnp.abs(x)))


def _dot(a, b):
    return jnp.dot(a.astype(BF16), b.astype(BF16), preferred_element_type=F32)


def _dot_nt(a, b):
    return lax.dot_general(a.astype(BF16), b.astype(BF16), (((1,), (1,)), ((), ())),
                           preferred_element_type=F32)


def _dot_tn(a, b):
    return lax.dot_general(a.astype(BF16), b.astype(BF16), (((0,), (0,)), ((), ())),
                           preferred_element_type=F32)


def _dot_split(a, b):
    a_hi = a.astype(BF16)
    b_hi = b.astype(BF16)
    a_lo = (a - a_hi.astype(F32)).astype(BF16)
    b_lo = (b - b_hi.astype(F32)).astype(BF16)
    dot = functools.partial(jnp.dot, preferred_element_type=F32)
    return dot(a_hi, b_hi) + dot(a_hi, b_lo) + dot(a_lo, b_hi)


def _dot_hi(a, b):
    return jnp.dot(a, b, preferred_element_type=F32, precision=HIGHEST)


def _rms(x, gain):
    return x * lax.rsqrt(jnp.mean(x * x, axis=-1, keepdims=True) + EPS) * gain


def _mod_row(i, tile):
    n_ctx = T_CTX // tile
    return jnp.where(i < n_ctx, 0, 1 + (i - n_ctx) // (DEC_SEQ // tile))


def _mod_kernel(c_ref, w_ref, b_ref, o_ref):
    o_ref[0] = _dot(_silu(c_ref[...]), w_ref[0]) + b_ref[0]


def _modulation(cvec, w_mod, b_mod):
    n_tile = 1536
    return pl.pallas_call(
        _mod_kernel,
        grid=(DEPTH, 6 * D_MODEL // n_tile),
        in_specs=[
            pl.BlockSpec((MOD_ROWS, D_MODEL), lambda l, n: (0, 0)),
            pl.BlockSpec((1, D_MODEL, n_tile), lambda l, n: (l, 0, n)),
            pl.BlockSpec((1, 1, n_tile), lambda l, n: (l, 0, n)),
        ],
        out_specs=pl.BlockSpec((1, MOD_ROWS, n_tile), lambda l, n: (l, 0, n)),
        out_shape=jax.ShapeDtypeStruct((DEPTH, MOD_ROWS, 6 * D_MODEL), F32),
        compiler_params=_params(("arbitrary", "arbitrary")),
        name="modulation",
    )(cvec, w_mod, b_mod.reshape(DEPTH, 1, 6 * D_MODEL))


def _inproj_kernel(x_ref, mod_ref, gain_ref, w_ref, o_ref, *, n_chunk):
    m = mod_ref[0]
    h = _rms(x_ref[...], gain_ref[...]) * (1.0 + m[1:2]) + m[0:1]
    hb = h.astype(BF16)
    n_cols = o_ref.shape[1]
    for c0 in range(0, n_cols, n_chunk):
        c1 = min(c0 + n_chunk, n_cols)
        o_ref[:, c0:c1] = jnp.dot(hb, w_ref[:, c0:c1], preferred_element_type=F32)


def _inproj(x, mod, gain, w):
    tile = 512
    n_cols = w.shape[1]
    return pl.pallas_call(
        functools.partial(_inproj_kernel, n_chunk=512),
        grid=(T_ALL // tile,),
        in_specs=[
            pl.BlockSpec((tile, D_MODEL), lambda i: (i, 0)),
            pl.BlockSpec((1, 6, D_MODEL), lambda i: (_mod_row(i, tile), 0, 0)),
            pl.BlockSpec((1, D_MODEL), lambda i: (0, 0)),
            pl.BlockSpec((D_MODEL, n_cols), lambda i: (0, 0)),
        ],
        out_specs=pl.BlockSpec((tile, n_cols), lambda i: (i, 0)),
        out_shape=jax.ShapeDtypeStruct((T_ALL, n_cols), F32),
        compiler_params=_params(("arbitrary",)),
        name="inproj",
    )(x, mod, gain.reshape(1, D_MODEL), w)


def _conv3(x, w, first, last, seq_rows):
    xp = jnp.where(first, 0.0, pltpu.roll(x, 1, 0))
    xn = jnp.where(last, 0.0, pltpu.roll(x, seq_rows - 1, 0))
    return xp * w[0:1] + x * w[1:2] + xn * w[2:3]


def _dn_kernel(*refs, seq, use_s0, write_state):
    refs = list(refs)
    q_ref, k_ref, v_ref, z_ref, ba_ref, cwq_ref, cwk_ref, cwv_ref, par_ref, norm_ref = refs[:10]
    pos = 10
    s0_ref = None
    if use_s0:
        s0_ref = refs[pos]
        pos += 1
    o_ref = refs[pos]
    pos += 1
    sfin_ref = None
    if write_state:
        sfin_ref = refs[pos]
        pos += 1
    q_s, k_s, v_s, of_s, ob_s, g_s, gr_s, beta_s, st_s = refs[pos:]

    C = DN_CHUNK
    n_chunks = seq // C
    hd = HEAD_DIM

    rows = lax.broadcasted_iota(jnp.int32, (seq, 1), 0)
    first = rows == 0
    last = rows == seq - 1
    for h in range(DN_HEADS):
        hs = slice(h * hd, (h + 1) * hd)
        q = _silu(_conv3(q_ref[:, hs], cwq_ref[:, hs], first, last, seq))
        k = _silu(_conv3(k_ref[:, hs], cwk_ref[:, hs], first, last, seq))
        v = _silu(_conv3(v_ref[:, hs], cwv_ref[:, hs], first, last, seq))
        q_s[:, hs] = q * lax.rsqrt(jnp.sum(q * q, axis=-1, keepdims=True) + EPS) * (hd ** -0.5)
        k_s[:, hs] = k * lax.rsqrt(jnp.sum(k * k, axis=-1, keepdims=True) + EPS)
        v_s[:, hs] = v

    ba = ba_ref[...]
    beta_s[...] = _sigmoid(ba)
    g = -jnp.exp(par_ref[0:1]) * _softplus(ba + par_ref[1:2])
    lane = lax.broadcasted_iota(jnp.int32, (1, LANES), 1)
    fwd_lane = lane < 8 + DN_HEADS
    ci = lax.broadcasted_iota(jnp.int32, (C, C), 0)
    cj = lax.broadcasted_iota(jnp.int32, (C, C), 1)
    tril1 = (ci >= cj).astype(F32)
    triu1 = (ci <= cj).astype(F32)
    for n in range(n_chunks):
        gn = g[n * C:(n + 1) * C]
        gc = jnp.where(fwd_lane, _dot_hi(tril1, gn), _dot_hi(triu1, gn))
        g_s[n * C:(n + 1) * C] = gc
        gr_s[n] = gc.T

    if use_s0:
        for d in range(2):
            for h in range(DN_HEADS):
                st_s[d * DN_HEADS + h] = s0_ref[0, d, h]
    else:
        st_s[...] = jnp.zeros_like(st_s)

    def step(t, carry):
        loaded = []
        for d in range(2):
            n = t if d == 0 else n_chunks - 1 - t
            r0 = pl.multiple_of(n * C, C)
            r_last = r0 + (C - 1 if d == 0 else 0)
            for h in range(DN_HEADS):
                p = d * DN_HEADS + h
                hs = slice(h * hd, (h + 1) * hd)
                loaded.append((
                    q_s[pl.ds(r0, C), hs], k_s[pl.ds(r0, C), hs], v_s[pl.ds(r0, C), hs],
                    g_s[pl.ds(r0, C), 8 + p:9 + p],
                    gr_s[n, 8 + p:9 + p, :],
                    beta_s[pl.ds(r0, C), p:p + 1],
                    g_s[pl.ds(r_last, 1), 8 + p:9 + p],
                    st_s[p], r0))
        n_p = len(loaded)
        qn, kn, vn, gcol, grow, bcol, glast, s, r0 = (list(col) for col in zip(*loaded))
        incl = [(ci >= cj) if p < DN_HEADS else (ci <= cj) for p in range(n_p)]
        strict = [(ci > cj) if p < DN_HEADS else (ci < cj) for p in range(n_p)]
        decay = [jnp.exp(jnp.where(incl[p], gcol[p] - grow[p], -jnp.inf)) for p in range(n_p)]
        kb = [kn[p] * bcol[p] for p in range(n_p)]
        y = [jnp.where(strict[p], -(_dot_nt(kb[p], kn[p]) * decay[p]), 0.0) for p in range(n_p)]
        attn = [_dot_nt(qn[p], kn[p]) * decay[p] for p in range(n_p)]
        q = [_dot_split(y[p], y[p]) for p in range(n_p)]
        for _ in range(4):
            both = [_dot_split(jnp.concatenate([q[p], y[p]], axis=0), q[p]) for p in range(n_p)]
            y = [y[p] + q[p] + both[p][C:] for p in range(n_p)]
            q = [both[p][:C] for p in range(n_p)]
        yq = [_dot_split(y[p], q[p]) for p in range(n_p)]
        y = [y[p] + q[p] + yq[p] for p in range(n_p)]
        egc = [jnp.exp(gcol[p]) for p in range(n_p)]
        rhs = [jnp.concatenate([vn[p] * bcol[p], kb[p] * egc[p]], axis=1) for p in range(n_p)]
        sol = [rhs[p] + _dot_split(y[p], rhs[p]) for p in range(n_p)]
        v_new = [sol[p][:, :hd] - _dot(sol[p][:, hd:], s[p]) for p in range(n_p)]
        o_old = [_dot(qn[p] * egc[p], s[p]) for p in range(n_p)]
        o = [o_old[p] + _dot(attn[p], v_new[p]) for p in range(n_p)]
        s_new = [s[p] * jnp.exp(glast[p]) + _dot_tn(kn[p] * jnp.exp(glast[p] - gcol[p]), v_new[p])
                 for p in range(n_p)]
        for p in range(n_p):
            d, h = divmod(p, DN_HEADS)
            st_s[p] = s_new[p]
            (of_s if d == 0 else ob_s)[pl.ds(r0[p], C), h * hd:(h + 1) * hd] = o[p]
        return carry

    lax.fori_loop(0, n_chunks, step, 0)

    for h in range(DN_HEADS):
        hs = slice(h * hd, (h + 1) * hd)
        o_ref[:, hs] = _rms(of_s[:, hs] + ob_s[:, hs], norm_ref[...]) * _silu(z_ref[:, hs])
    if write_state:
        for d in range(2):
            for h in range(DN_HEADS):
                sfin_ref[0, d, h] = st_s[d * DN_HEADS + h]


def _deltanet(proj, conv_w, par, norm, s0, prev_out, *, seq, n_seq, row0):
    use_s0 = s0 is not None
    write_state = not use_s0
    blk0 = row0 // seq
    half = MIX_HALF
    in_specs = [
        pl.BlockSpec((seq, half), lambda b: (blk0 + b, 0)),
        pl.BlockSpec((seq, half), lambda b: (blk0 + b, 1)),
        pl.BlockSpec((seq, half), lambda b: (blk0 + b, 2)),
        pl.BlockSpec((seq, half), lambda b: (blk0 + b, 6)),
        pl.BlockSpec((seq, LANES), lambda b: (blk0 + b, 28)),
        pl.BlockSpec((3, half), lambda b: (0, 0)),
        pl.BlockSpec((3, half), lambda b: (0, 1)),
        pl.BlockSpec((3, half), lambda b: (0, 2)),
        pl.BlockSpec((8, LANES), lambda b: (0, 0)),
        pl.BlockSpec((1, HEAD_DIM), lambda b: (0, 0)),
    ]
    args = [proj, proj, proj, proj, proj, conv_w, conv_w, conv_w, par, norm]
    if use_s0:
        in_specs.append(pl.BlockSpec((1, 2, DN_HEADS, HEAD_DIM, HEAD_DIM), lambda b: (b, 0, 0, 0, 0)))
        args.append(s0)
    aliases = {}
    if prev_out is not None:
        in_specs.append(pl.BlockSpec(memory_space=pl.ANY))
        args.append(prev_out)
        aliases = {len(args) - 1: 0}
    out_specs = [pl.BlockSpec((seq, half), lambda b: (blk0 + b, 0))]
    out_shape = [jax.ShapeDtypeStruct((T_ALL, half), F32)]
    if write_state:
        out_specs.append(pl.BlockSpec((1, 2, DN_HEADS, HEAD_DIM, HEAD_DIM), lambda b: (b, 0, 0, 0, 0)))
        out_shape.append(jax.ShapeDtypeStruct((n_seq, 2, DN_HEADS, HEAD_DIM, HEAD_DIM), F32))

    def body(*refs):
        refs = list(refs)
        if prev_out is not None:
            n_in = len(args)
            del refs[n_in - 1]
        _dn_kernel(*refs, seq=seq, use_s0=use_s0, write_state=write_state)

    n_chunks = seq // DN_CHUNK
    res = pl.pallas_call(
        body,
        grid=(n_seq,),
        in_specs=in_specs,
        out_specs=out_specs,
        out_shape=out_shape,
        scratch_shapes=[
            pltpu.VMEM((seq, half), F32),
            pltpu.VMEM((seq, half), F32),
            pltpu.VMEM((seq, half), F32),
            pltpu.VMEM((seq, half), F32),
            pltpu.VMEM((seq, half), F32),
            pltpu.VMEM((seq, LANES), F32),
            pltpu.VMEM((n_chunks, LANES, DN_CHUNK), F32),
            pltpu.VMEM((seq, LANES), F32),
            pltpu.VMEM((2 * DN_HEADS, HEAD_DIM, HEAD_DIM), F32),
        ],
        input_output_aliases=aliases,
        compiler_params=_params(("arbitrary",)),
        name="deltanet_%d" % seq,
    )(*args)
    return res


@functools.lru_cache(maxsize=None)
def _dft_tables(seq):
    k = np.arange(seq, dtype=np.int64)[:, None]
    s = np.arange(seq, dtype=np.int64)[None, :]
    ang = np.pi * ((k * s) % (2 * seq)).astype(np.float64) / seq
    cos = np.cos(ang)
    sin = np.sin(ang)
    fwd = np.concatenate([cos, -sin], axis=0)
    fwd[seq, :] = np.where(np.arange(seq) % 2 == 0, 1.0, -1.0)
    wk = np.full((seq,), 2.0)
    wk[0] = 1.0
    inv = np.concatenate([cos.T * wk[None, :], -2.0 * sin.T], axis=1) / (2.0 * seq)
    inv[:, seq] = np.where(np.arange(seq) % 2 == 0, 1.0, -1.0) / (2.0 * seq)
    return fwd.astype(np.float32), inv.astype(np.float32)


def _hy_filter_kernel(feat_ref, w1_ref, b1_ref, f1_ref, w2_ref, b2_ref, f2_ref, w3f_ref, w3b_ref,
                      dec_ref, fwd_ref, o_ref, *, seq):
    z = jnp.sin(f1_ref[...] * (_dot(feat_ref[...], w1_ref[...]) + b1_ref[...]))
    z = jnp.sin(f2_ref[...] * (_dot(z, w2_ref[...]) + b2_ref[...]))
    dec = dec_ref[...]
    h_fwd = _dot(z, w3f_ref[...]) * dec
    h_bwd = _dot(z, w3b_ref[...]) * dec
    rows = lax.broadcasted_iota(jnp.int32, (seq, 1), 0)
    h_bwd = jnp.where(rows == 0, 0.0, h_bwd)
    spec = _dot_split(fwd_ref[...], jnp.concatenate([h_fwd, h_bwd], axis=1))
    g = h_fwd.shape[1]
    srow = lax.broadcasted_iota(jnp.int32, (2 * seq, 1), 0)
    sign = jnp.where(srow <= seq, 1.0, -1.0)
    o_ref[0] = spec[:, :g] + sign * spec[:, g:]


def _hyena_filters(seq, w1, b1, f1, w2, b2, f2, w3):
    t = jnp.linspace(0.0, 1.0, seq, dtype=F32)[:, None]
    w = (2.0 * math.pi / seq) * jnp.arange(seq, dtype=F32)[:, None]
    f = jnp.linspace(1e-4, HY_BANDS - 1, HY_BANDS, dtype=F32)[None, :]
    feats = jnp.concatenate([t, jnp.cos(f * w), -jnp.sin(f * w)], axis=-1)
    n_feat = feats.shape[1]
    feats = jnp.pad(feats, ((0, 0), (0, LANES - n_feat)))
    w1p = jnp.pad(w1, ((0, LANES - n_feat), (0, 0)))
    deltas = jnp.abs(jnp.linspace(math.log(HY_TARGET) / HY_DECAY_PCT_LONG,
                                  math.log(HY_TARGET) / HY_DECAY_PCT_SHORT, HY_WIDTH, dtype=F32))
    dec = jnp.exp(-t * deltas)
    fwd, _ = _dft_tables(seq)
    g = HY_GROUP
    n_g = HY_WIDTH // g
    small = lambda shape: pl.BlockSpec(shape, lambda o, c: (0,) * len(shape))
    return pl.pallas_call(
        functools.partial(_hy_filter_kernel, seq=seq),
        grid=(2, n_g),
        in_specs=[
            small((seq, LANES)), small((LANES, HY_FFN)), small((1, HY_FFN)), small((1, HY_FFN)),
            small((HY_FFN, HY_FFN)), small((1, HY_FFN)), small((1, HY_FFN)),
            pl.BlockSpec((HY_FFN, g), lambda o, c: (0, (2 * o) * n_g + c)),
            pl.BlockSpec((HY_FFN, g), lambda o, c: (0, (2 * o + 1) * n_g + c)),
            pl.BlockSpec((seq, g), lambda o, c: (0, c)),
            small((2 * seq, seq)),
        ],
        out_specs=pl.BlockSpec((1, 2 * seq, g), lambda o, c: (o, 0, c)),
        out_shape=jax.ShapeDtypeStruct((2, 2 * seq, HY_WIDTH), F32),
        compiler_params=_params(("arbitrary", "arbitrary")),
        name="hyena_filters_%d" % seq,
    )(feats, w1p, b1.reshape(1, -1), f1.reshape(1, -1), w2, b2.reshape(1, -1), f2.reshape(1, -1),
      w3, w3, dec, jnp.asarray(fwd))


def _hy_kernel(x1_ref, x2_ref, z_ref, cw1_ref, cw2_ref, cwz_ref, cb1_ref, cb2_ref, cbz_ref,
               spec_ref, bias_ref, fwd_ref, inv_ref, o_ref, *, seq, n_par):
    rows = lax.broadcasted_iota(jnp.int32, (seq, 1), 0)
    first = rows == 0
    last = rows == seq - 1
    par = range(n_par)
    rs = [slice(i * seq, (i + 1) * seq) for i in par]
    gates = [(_conv3(x1_ref[rs[i], :], cw1_ref[...], first, last, seq) + cb1_ref[...],
              _conv3(x2_ref[rs[i], :], cw2_ref[...], first, last, seq) + cb2_ref[...]) for i in par]
    z = [_conv3(z_ref[rs[i], :], cwz_ref[...], first, last, seq) + cbz_ref[...] for i in par]
    for o in range(2):
        zf = [jnp.dot(fwd_ref[...], z[i].astype(BF16), preferred_element_type=F32) for i in par]
        hre, him = spec_ref[o, :seq, :], spec_ref[o, seq:, :]
        prod = []
        for i in par:
            re, im = zf[i][:seq], zf[i][seq:]
            yre = re * hre - jnp.where(first, 0.0, im * him)
            yim = jnp.where(first, im * him, re * him + im * hre)
            prod.append(jnp.concatenate([yre, yim], axis=0).astype(BF16))
        y = [jnp.dot(inv_ref[...], prod[i], preferred_element_type=F32) for i in par]
        z = [gates[i][o] * (y[i] + z[i] * bias_ref[o:o + 1, :]) for i in par]
    for i in par:
        o_ref[rs[i], :] = z[i]


def _hyena(proj, conv_w, conv_b, spec, bias, prev_out, *, seq, n_seq, row0):
    g = HY_GROUP
    n_g = HY_WIDTH // g
    n_par = 2
    blk_rows = n_par * seq
    blk0 = row0 // blk_rows
    col0 = 3 * MIX_HALF // g
    fwd, inv = _dft_tables(seq)
    conv_b = conv_b.reshape(1, -1)

    def pcol(part):
        return pl.BlockSpec((blk_rows, g), lambda c, b: (blk0 + b, col0 + part * n_g + c))

    def wcol(rows_, part):
        return pl.BlockSpec((rows_, g), lambda c, b: (0, part * n_g + c))

    in_specs = [pcol(0), pcol(1), pcol(2), wcol(3, 0), wcol(3, 1), wcol(3, 2),
                wcol(1, 0), wcol(1, 1), wcol(1, 2),
                pl.BlockSpec((2, 2 * seq, g), lambda c, b: (0, 0, c)),
                pl.BlockSpec((2, g), lambda c, b: (0, c)),
                pl.BlockSpec((2 * seq, seq), lambda c, b: (0, 0), pipeline_mode=pl.Buffered(1)),
                pl.BlockSpec((seq, 2 * seq), lambda c, b: (0, 0), pipeline_mode=pl.Buffered(1))]
    args = [proj, proj, proj, conv_w, conv_w, conv_w, conv_b, conv_b, conv_b, spec, bias,
            jnp.asarray(fwd).astype(BF16), jnp.asarray(inv).astype(BF16)]
    aliases = {}
    if prev_out is not None:
        in_specs.append(pl.BlockSpec(memory_space=pl.ANY))
        args.append(prev_out)
        aliases = {len(args) - 1: 0}

    def body(*refs):
        refs = list(refs)
        if prev_out is not None:
            del refs[len(args) - 1]
        _hy_kernel(*refs, seq=seq, n_par=n_par)

    return pl.pallas_call(
        body,
        grid=(n_g, n_seq // n_par),
        in_specs=in_specs,
        out_specs=pl.BlockSpec((blk_rows, g), lambda c, b: (blk0 + b, c)),
        out_shape=jax.ShapeDtypeStruct((T_ALL, HY_WIDTH), F32),
        input_output_aliases=aliases,
        compiler_params=_params(("arbitrary", "arbitrary")),
        name="hyena_%d" % seq,
    )(*args)


def _softmax_av(s, v, sink):
    m = jnp.max(s, axis=-1, keepdims=True)
    if sink is not None:
        m = jnp.maximum(m, sink)
    e = jnp.exp(s - m)
    den = jnp.sum(e, axis=-1, keepdims=True)
    if sink is not None:
        den = den + jnp.exp(sink - m)
    return _dot(e, v) / den


def _attn_ctx_kernel(p_ref, qn_ref, kn_ref, sink_ref, o_ref, kc_ref, vc_ref, kd_ref, vd_ref):
    hd = HEAD_DIM
    scale = hd ** -0.5
    kv_w = KV_HEADS * hd
    q_w = KV_HEADS * Q_GROUP * hd
    vc_ref[0] = p_ref[:, q_w + kv_w:q_w + 2 * kv_w]
    base_d = q_w + 2 * kv_w
    kd_ref[0] = p_ref[:, base_d + q_w:base_d + q_w + kv_w]
    vd_ref[0] = p_ref[:, base_d + q_w + kv_w:base_d + q_w + 2 * kv_w]
    heads = []
    for kv in range(KV_HEADS):
        ks = slice(kv * hd, (kv + 1) * hd)
        k = _rms(p_ref[:, q_w + kv * hd:q_w + (kv + 1) * hd], kn_ref[...])
        kc_ref[0, :, ks] = k
        v = p_ref[:, q_w + kv_w + kv * hd:q_w + kv_w + (kv + 1) * hd]
        for g in range(Q_GROUP):
            h = kv * Q_GROUP + g
            q = _rms(p_ref[:, h * hd:(h + 1) * hd], qn_ref[...]) * scale
            heads.append((_dot_nt(q, k), v, None, h * hd))
        k = p_ref[:, base_d + q_w + kv * hd:base_d + q_w + (kv + 1) * hd]
        v = p_ref[:, base_d + q_w + kv_w + kv * hd:base_d + q_w + kv_w + (kv + 1) * hd]
        for g in range(Q_GROUP):
            h = kv * Q_GROUP + g
            q = p_ref[:, base_d + h * hd:base_d + (h + 1) * hd] * scale
            heads.append((_dot_nt(q, k), v, sink_ref[0:1, h:h + 1], q_w + h * hd))
    for logits, v, sink, col in heads:
        o_ref[:, col:col + hd] = _softmax_av(logits, v, sink)


def _attn_ctx(proj, q_norm, k_norm, sink):
    kv_w = KV_HEADS * HEAD_DIM
    cache = jax.ShapeDtypeStruct((BATCH, SEQ, kv_w), F32)
    cache_spec = pl.BlockSpec((1, SEQ, kv_w), lambda b: (b, 0, 0))
    return pl.pallas_call(
        _attn_ctx_kernel,
        grid=(BATCH,),
        in_specs=[
            pl.BlockSpec((SEQ, ODD_COLS), lambda b: (b, 0)),
            pl.BlockSpec((1, HEAD_DIM), lambda b: (0, 0)),
            pl.BlockSpec((1, HEAD_DIM), lambda b: (0, 0)),
            pl.BlockSpec((1, LANES), lambda b: (0, 0)),
        ],
        out_specs=[pl.BlockSpec((SEQ, D_MODEL), lambda b: (b, 0)),
                   cache_spec, cache_spec, cache_spec, cache_spec],
        out_shape=[jax.ShapeDtypeStruct((T_ALL, D_MODEL), F32), cache, cache, cache, cache],
        compiler_params=_params(("arbitrary",)),
        name="attn_ctx",
    )(proj, q_norm.reshape(1, -1), k_norm.reshape(1, -1), sink)


def _rope(x, cos, sin_signed):
    lane = lax.broadcasted_iota(jnp.int32, (1, HEAD_DIM), 1)
    lo = (lane % 64) < 32
    partner = jnp.where(lo, pltpu.roll(x, HEAD_DIM - 32, 1), pltpu.roll(x, 32, 1))
    return x * cos + partner * sin_signed


def _attn_lat_kernel(p_ref, ckc_ref, cvc_ref, ckd_ref, cvd_ref, qn_ref, kn_ref, sink_ref,
                     cos_ref, sin_ref, o_ref, k_s, v_s, q_s, *, q_blk):
    hd = HEAD_DIM
    scale = hd ** -0.5
    seq = DEC_SEQ
    kv_w = KV_HEADS * hd
    q_w = KV_HEADS * Q_GROUP * hd
    n_keys = PAST_LEN + seq
    n_blk = seq // q_blk
    cos = cos_ref[...]
    sin = sin_ref[...]
    kpos = lax.broadcasted_iota(jnp.int32, (1, n_keys), 1) - PAST_LEN
    qrow = lax.broadcasted_iota(jnp.int32, (Q_GROUP * q_blk, 1), 0) % q_blk
    for mixer in range(2):
        base = mixer * (q_w + 2 * kv_w)
        ck_ref, cv_ref = (ckc_ref, cvc_ref) if mixer == 0 else (ckd_ref, cvd_ref)
        for kv in range(KV_HEADS):
            ks = slice(kv * hd, (kv + 1) * hd)
            k = p_ref[:, base + q_w + kv * hd:base + q_w + (kv + 1) * hd]
            if mixer == 0:
                k = _rms(k, kn_ref[...])
            k_s[:PAST_LEN, :] = ck_ref[0, :, ks].astype(BF16)
            k_s[PAST_LEN:, :] = _rope(k, cos, sin).astype(BF16)
            v_s[:PAST_LEN, :] = cv_ref[0, :, ks].astype(BF16)
            v_s[PAST_LEN:, :] = p_ref[:, base + q_w + kv_w + kv * hd:
                                      base + q_w + kv_w + (kv + 1) * hd].astype(BF16)
            for g in range(Q_GROUP):
                h = kv * Q_GROUP + g
                q = p_ref[:, base + h * hd:base + (h + 1) * hd]
                if mixer == 0:
                    q = _rms(q, qn_ref[...])
                q = _rope(q, cos, sin) * scale
                for b in range(n_blk):
                    q_s[b, g * q_blk:(g + 1) * q_blk, :] = q[b * q_blk:(b + 1) * q_blk].astype(BF16)

            def blk(i, carry):
                blocks = [2 * i, 2 * i + 1]
                logits = []
                for b in blocks:
                    s = lax.dot_general(q_s[b], k_s[...], (((1,), (1,)), ((), ())),
                                        preferred_element_type=F32)
                    if mixer == 1:
                        qpos = b * q_blk + qrow
                        ok = (kpos < 0) | (jnp.abs(kpos - qpos) <= WINDOW)
                        s = jnp.where(ok, s, NEG_BIG)
                    logits.append(s)
                for b, s in zip(blocks, logits):
                    r0 = pl.multiple_of(b * q_blk, q_blk)
                    for g in range(Q_GROUP):
                        h = kv * Q_GROUP + g
                        sink = sink_ref[0:1, h:h + 1] if mixer == 1 else None
                        o = _softmax_av(s[g * q_blk:(g + 1) * q_blk], v_s[...], sink)
                        o_ref[pl.ds(r0, q_blk),
                              mixer * q_w + h * hd:mixer * q_w + (h + 1) * hd] = o
                return carry

            lax.fori_loop(0, n_blk // 2, blk, 0)


def _attn_lat(proj, ck_c, cv_c, ck_d, cv_d, q_norm, k_norm, sink, cos, sin, prev_out):
    kv_w = KV_HEADS * HEAD_DIM
    q_blk = 256
    blk0 = T_CTX // DEC_SEQ
    cache_spec = pl.BlockSpec((1, PAST_LEN, kv_w), lambda b: (b, 0, 0))
    row = lambda n: pl.BlockSpec((1, n), lambda b: (0, 0))
    tab = pl.BlockSpec((DEC_SEQ, HEAD_DIM), lambda b: (0, 0))

    def body(*refs):
        refs = list(refs)
        del refs[10]
        _attn_lat_kernel(*refs, q_blk=q_blk)

    return pl.pallas_call(
        body,
        grid=(DEC_BATCH,),
        in_specs=[pl.BlockSpec((DEC_SEQ, ODD_COLS), lambda b: (blk0 + b, 0)),
                  cache_spec, cache_spec, cache_spec, cache_spec,
                  row(HEAD_DIM), row(HEAD_DIM), row(LANES), tab, tab,
                  pl.BlockSpec(memory_space=pl.ANY)],
        out_specs=pl.BlockSpec((DEC_SEQ, D_MODEL), lambda b: (blk0 + b, 0)),
        out_shape=jax.ShapeDtypeStruct((T_ALL, D_MODEL), F32),
        scratch_shapes=[pltpu.VMEM((PAST_LEN + DEC_SEQ, HEAD_DIM), BF16),
                        pltpu.VMEM((PAST_LEN + DEC_SEQ, HEAD_DIM), BF16),
                        pltpu.VMEM((DEC_SEQ // q_blk, Q_GROUP * q_blk, HEAD_DIM), BF16)],
        input_output_aliases={10: 0},
        compiler_params=_params(("arbitrary",)),
        name="attn_lat",
    )(proj, ck_c, cv_c, ck_d, cv_d, q_norm.reshape(1, -1), k_norm.reshape(1, -1), sink, cos, sin,
      prev_out)


def _rope_tables(seq):
    rows = seq // GRID_W
    row = jnp.repeat(jnp.arange(rows), GRID_W).astype(F32)
    col = (jnp.arange(rows * GRID_W) % GRID_W).astype(F32)
    half = HEAD_DIM // 2
    inv = ROPE_THETA ** (-jnp.arange(0, half, 2, dtype=F32) / half)
    ar = row[:, None] * inv
    ac = col[:, None] * inv
    cos = jnp.concatenate([jnp.cos(ar), jnp.cos(ar), jnp.cos(ac), jnp.cos(ac)], axis=-1)
    sin = jnp.concatenate([-jnp.sin(ar), jnp.sin(ar), -jnp.sin(ac), jnp.sin(ac)], axis=-1)
    return cos, sin


def _ffn_kernel(*refs, n_mix, tile, final):
    refs = list(refs)
    x_ref = refs[0]
    mix_refs = refs[1:1 + n_mix]
    wo_refs = refs[1 + n_mix:1 + 2 * n_mix]
    pos = 1 + 2 * n_mix
    mod_ref, gain_ref, wup_ref, cw_ref, cb_ref, wdn_ref = refs[pos:pos + 6]
    pos += 6
    fin_ref = None
    if final:
        fin_ref = refs[pos]
        pos += 1
    o_ref = refs[pos]
    h_s, acc_s = refs[pos + 1:]

    m = mod_ref[0]
    mixed = None
    for mr, wr in zip(mix_refs, wo_refs):
        part = jnp.dot(mr[...].astype(BF16), wr[...], preferred_element_type=F32)
        mixed = part if mixed is None else mixed + part
    x1 = x_ref[...] + m[2:3] * mixed
    o_ref[...] = x1
    h_s[...] = (_rms(x1, gain_ref[...]) * (1.0 + m[4:5]) + m[3:4]).astype(BF16)
    acc_s[...] = jnp.zeros_like(acc_s)

    seq_mask = jnp.where(pl.program_id(0) < T_CTX // tile, SEQ - 1, DEC_SEQ - 1)
    pos_in_seq = lax.broadcasted_iota(jnp.int32, (tile, 1), 0) & seq_mask
    first = pos_in_seq == 0
    last = pos_in_seq == seq_mask

    def up(c):
        cols_a = slice(c * FF_CHUNK, (c + 1) * FF_CHUNK)
        cols_b = slice(D_FF + c * FF_CHUNK, D_FF + (c + 1) * FF_CHUNK)
        h = h_s[...]
        return (jnp.dot(h, wup_ref[:, cols_a], preferred_element_type=F32),
                jnp.dot(h, wup_ref[:, cols_b], preferred_element_type=F32))

    ab = up(0)
    for c in range(N_FF_CHUNKS):
        nxt = up(c + 1) if c + 1 < N_FF_CHUNKS else None
        cols = slice(c * FF_CHUNK, (c + 1) * FF_CHUNK)
        a = _conv3(ab[0], cw_ref[:, cols], first, last, tile) + cb_ref[:, cols]
        hid = (_silu(a) * ab[1]).astype(BF16)
        acc_s[...] += jnp.dot(hid, wdn_ref[cols, :], preferred_element_type=F32)
        ab = nxt
    x2 = o_ref[...] + m[5:6] * acc_s[...]
    if final:
        x2 = _rms(x2, fin_ref[...])
    o_ref[...] = x2


def _ffn(x, mixes, w_outs, mod, gain, w_up, conv_w, conv_b, w_down, final_gain):
    tile = 1024
    n_mix = len(mixes)
    final = final_gain is not None
    const = lambda shape: pl.BlockSpec(shape, lambda i: (0,) * len(shape),
                                       pipeline_mode=pl.Buffered(1))
    in_specs = [pl.BlockSpec((tile, D_MODEL), lambda i: (i, 0))]
    in_specs += [pl.BlockSpec((tile, mx.shape[1]), lambda i: (i, 0)) for mx in mixes]
    in_specs += [const(w.shape) for w in w_outs]
    in_specs += [
        pl.BlockSpec((1, 6, D_MODEL), lambda i: (_mod_row(i, tile), 0, 0)),
        const((1, D_MODEL)),
        const((D_MODEL, 2 * D_FF)),
        const((3, D_FF)),
        const((1, D_FF)),
        const((D_FF, D_MODEL)),
    ]
    args = [x, *mixes, *w_outs, mod, gain.reshape(1, D_MODEL), w_up, conv_w, conv_b, w_down]
    if final:
        in_specs.append(const((1, D_MODEL)))
        args.append(final_gain.reshape(1, D_MODEL))
    return pl.pallas_call(
        functools.partial(_ffn_kernel, n_mix=n_mix, tile=tile, final=final),
        grid=(T_ALL // tile,),
        in_specs=in_specs,
        out_specs=pl.BlockSpec((tile, D_MODEL), lambda i: (i, 0)),
        out_shape=jax.ShapeDtypeStruct((T_ALL, D_MODEL), F32),
        scratch_shapes=[pltpu.VMEM((tile, D_MODEL), BF16), pltpu.VMEM((tile, D_MODEL), F32)],
        compiler_params=_params(("arbitrary",)),
        name="outproj_ffn",
    )(*args)


def _even_weight(w):
    ha = MIX_HALF
    n_ba = 4 * DN_HEADS
    main = jnp.concatenate([w[:, :3 * ha], w[:, 4 * ha + n_ba:], w[:, 3 * ha:4 * ha]], axis=1)
    ba = jnp.pad(w[:, 4 * ha:4 * ha + n_ba], ((0, 0), (0, LANES - n_ba)))
    return jnp.concatenate([main, ba], axis=1).astype(BF16)


def kernel(x_prompt, x_sample, state_dn, cache_k_c, cache_v_c, cache_k_d, cache_v_d, c, c_ctx,
           final_norm, w_mod, b_mod, norm_mix, norm_ffn, w_out, ffn_w_up, ffn_conv_w, ffn_conv_b,
           ffn_w_down, ev_w_in, dn_conv_w, dn_a_log, dn_dt_bias, dn_norm, hy_conv_w, hy_conv_b,
           hy_w1, hy_b1, hy_freq1, hy_w2, hy_b2, hy_freq2, hy_w3, hy_bias, od_w_in, c_q_norm,
           c_k_norm, d_sink):
    x = jnp.concatenate([x_prompt.reshape(T_CTX, D_MODEL), x_sample.reshape(T_LAT, D_MODEL)], axis=0)
    cvec = jnp.concatenate([c_ctx[None, :], c, jnp.zeros((MOD_ROWS - 1 - DEC_BATCH, D_MODEL), F32)], axis=0)
    mods = _modulation(cvec, w_mod, b_mod).reshape(DEPTH, MOD_ROWS, 6, D_MODEL)
    cos, sin = _rope_tables(DEC_SEQ)
    kv_w = KV_HEADS * HEAD_DIM

    states, k_c, v_c, k_d, v_d = [], [], [], [], []
    for i in range(DEPTH):
        j = i // 2
        mod = mods[i]
        if i % 2 == 0:
            proj = _inproj(x, mod, norm_mix[i], _even_weight(ev_w_in[j]))
            par = jnp.zeros((8, LANES), F32)
            par = par.at[0, 8:8 + 2 * DN_HEADS].set(dn_a_log[j].reshape(-1))
            par = par.at[1, 8:8 + 2 * DN_HEADS].set(dn_dt_bias[j].reshape(-1))
            norm = dn_norm[j].reshape(1, HEAD_DIM)
            o_mix, s_fin = _deltanet(proj, dn_conv_w[j], par, norm, None, None,
                                     seq=SEQ, n_seq=BATCH, row0=0)
            (o_mix,) = _deltanet(proj, dn_conv_w[j], par, norm, state_dn[:, j], o_mix,
                                 seq=DEC_SEQ, n_seq=DEC_BATCH, row0=T_CTX)
            states.append(s_fin)
            filt = (hy_w1[j], hy_b1[j], hy_freq1[j], hy_w2[j], hy_b2[j], hy_freq2[j], hy_w3[j])
            y_mix = _hyena(proj, hy_conv_w[j], hy_conv_b[j], _hyena_filters(SEQ, *filt), hy_bias[j],
                           None, seq=SEQ, n_seq=BATCH, row0=0)
            y_mix = _hyena(proj, hy_conv_w[j], hy_conv_b[j], _hyena_filters(DEC_SEQ, *filt),
                           hy_bias[j], y_mix, seq=DEC_SEQ, n_seq=DEC_BATCH, row0=T_CTX)
            mixes = [o_mix, y_mix]
            wo = w_out[i].astype(BF16)
            w_outs = [wo[:MIX_HALF], wo[MIX_HALF:]]
        else:
            proj = _inproj(x, mod, norm_mix[i], od_w_in[j].astype(BF16))
            sink = jnp.pad(d_sink[j].reshape(1, -1), ((0, 0), (0, LANES - d_sink.shape[1])))
            mix, kc, vc, kd, vd = _attn_ctx(proj, c_q_norm[j], c_k_norm[j], sink)
            k_c.append(kc)
            v_c.append(vc)
            k_d.append(kd)
            v_d.append(vd)
            flat = lambda t: t[:, j].reshape(DEC_BATCH, PAST_LEN, kv_w)
            mix = _attn_lat(proj, flat(cache_k_c), flat(cache_v_c), flat(cache_k_d), flat(cache_v_d),
                            c_q_norm[j], c_k_norm[j], sink, cos, sin, mix)
            mixes = [mix]
            w_outs = [w_out[i].astype(BF16)]
        x = _ffn(x, mixes, w_outs, mod, norm_ffn[i], ffn_w_up[i].astype(BF16), ffn_conv_w[i],
                 ffn_conv_b[i].reshape(1, D_FF), ffn_w_down[i].astype(BF16),
                 final_norm if i == DEPTH - 1 else None)

    y_prompt = x[:T_CTX].reshape(BATCH, SEQ, D_MODEL)
    y_sample = x[T_CTX:].reshape(DEC_BATCH, DEC_SEQ, D_MODEL)
    heads = lambda ts: jnp.stack(ts, axis=1).reshape(BATCH, len(ts), SEQ, KV_HEADS, HEAD_DIM)
    return (y_prompt, y_sample, jnp.stack(states, axis=1), heads(k_c), heads(v_c), heads(k_d),
            heads(v_d))
```
